```python
import jax, jax.numpy as jnp
from jax import lax
import numpy as np

D_MODEL = 2048
BATCH = 16
SEQ = 2048
DEPTH = 2
DEC_BATCH = 32
DEC_SEQ = 64
PAST_LEN = 4096

CHUNK = 64
N_META = 16
EPS = 1e-6
GLA_HEADS = 4
GLA_DK = D_MODEL // 16
GLA_DV = D_MODEL // 8
GLA_RANK = 16
GLA_TAU = 16.0
GLA_BLOCK = CHUNK
SB_HEAD_DIM = 128
SB_HEADS = D_MODEL // 256
SB_Q_BLOCK = 128

GLA_QK = GLA_HEADS * GLA_DK
GLA_V = GLA_HEADS * GLA_DV
SB_W = SB_HEADS * SB_HEAD_DIM
IN_SIZES = (GLA_QK, GLA_QK, GLA_V, GLA_V, GLA_RANK, SB_W, SB_W, SB_W, SB_W, D_MODEL, D_MODEL)
IN_SPLITS = tuple(int(s) for s in np.cumsum(IN_SIZES)[:-1])
IN_COLS = int(sum(IN_SIZES))

kernel_name = "gla_stickbreak_hybrid_stream_step"


def rms_norm(x, gain):
    xf = x.astype(jnp.float32)
    y = xf * lax.rsqrt(jnp.mean(xf * xf, axis=-1, keepdims=True) + EPS)
    return (y * gain.astype(jnp.float32)).astype(x.dtype)


def gla_recurrence(q, k, v, log_a, s0):
    B, T, H, DK = q.shape
    DV = v.shape[-1]
    C = GLA_BLOCK
    n = -(-T // C)
    pad = n * C - T

    def blocks(a):
        a = jnp.pad(a.astype(jnp.float32), ((0, 0), (0, pad), (0, 0), (0, 0)))
        return a.reshape(B, n, C, H, a.shape[-1]).transpose(1, 0, 3, 2, 4)

    qc, kc, vc, ac = blocks(q * DK ** -0.5), blocks(k), blocks(v), blocks(log_a)
    causal = jnp.tril(jnp.ones((C, C), dtype=bool))

    def step(S, blk):
        qi, ki, vi, ai = blk
        b = jnp.cumsum(ai, axis=2)
        diff = b[:, :, :, None, :] - b[:, :, None, :, :]
        decay = jnp.exp(jnp.where(causal[:, :, None], diff, -jnp.inf))
        scores = jnp.einsum('bhtk,bhsk,bhtsk->bhts', qi, ki, decay)
        o = (jnp.einsum('bhtk,bhkv->bhtv', qi * jnp.exp(b), S)
             + jnp.einsum('bhts,bhsv->bhtv', scores, vi))
        b_end = b[:, :, -1, :]
        S = (jnp.exp(b_end)[..., None] * S
             + jnp.einsum('bhsk,bhsv->bhkv', ki * jnp.exp(b_end[:, :, None, :] - b), vi))
        return S, o

    S, o = lax.scan(step, s0.astype(jnp.float32), (qc, kc, vc, ac))
    o = o.transpose(1, 0, 3, 2, 4).reshape(B, n * C, H, DV)[:, :T]
    return o, S


def stick_breaking_attention(q, k, v, q_offset):
    B, Tq, H, D = q.shape
    Tk = k.shape[1]
    qf = q.astype(jnp.float32) * D ** -0.5
    kf = k.astype(jnp.float32)
    vf = v.astype(jnp.float32)
    outs = []
    for start in range(0, Tq, SB_Q_BLOCK):
        stop = min(start + SB_Q_BLOCK, Tq)
        n_keys = max(1, min(q_offset + stop - 1, Tk))
        z = jnp.einsum('bqhd,bkhd->bhqk', qf[:, start:stop], kf[:, :n_keys])
        q_pos = q_offset + jnp.arange(start, stop)
        mask = jnp.arange(n_keys)[None, :] < q_pos[:, None]
        log_beta = jax.nn.log_sigmoid(z)
        log_om = jnp.where(mask, jax.nn.log_sigmoid(-z), 0.0)
        tail = lax.cumsum(log_om, axis=3, reverse=True) - log_om
        w = jnp.where(mask, jnp.exp(log_beta + tail), 0.0)
        outs.append(jnp.einsum('bhqk,bkhd->bqhd', w, vf[:, :n_keys]))
    return jnp.concatenate(outs, axis=1).astype(v.dtype)


def mixer_layer(x, k_past, v_past, s0, pre_gain, w_in, w_a2, b_a, gla_gain, w_up_gla, w_up_sb, w_o, post_gain):
    B, T, _ = x.shape
    h = rms_norm(x, pre_gain)
    (g_q, g_k, g_v, g_z, g_r, s_q, s_k, s_v, s_z, m_gla, m_sb) = jnp.split(h @ w_in, IN_SPLITS, axis=-1)

    log_a = jax.nn.log_sigmoid((g_r @ w_a2 + b_a).astype(jnp.float32)) / GLA_TAU
    o_gla, s_new = gla_recurrence(g_q.reshape(B, T, GLA_HEADS, GLA_DK),
                                  g_k.reshape(B, T, GLA_HEADS, GLA_DK),
                                  g_v.reshape(B, T, GLA_HEADS, GLA_DV),
                                  log_a.reshape(B, T, GLA_HEADS, GLA_DK), s0)
    o_gla = rms_norm(o_gla, gla_gain).astype(x.dtype).reshape(B, T, GLA_V)
    y_gla = (o_gla * jax.nn.silu(g_z)) @ w_up_gla

    k_new = s_k.reshape(B, T, SB_HEADS, SB_HEAD_DIM)
    v_new = s_v.reshape(B, T, SB_HEADS, SB_HEAD_DIM)
    if k_past is None:
        k_all, v_all, offset = k_new, v_new, 0
    else:
        k_all = jnp.concatenate([k_past.astype(k_new.dtype), k_new], axis=1)
        v_all = jnp.concatenate([v_past.astype(v_new.dtype), v_new], axis=1)
        offset = k_past.shape[1]
    o_sb = stick_breaking_attention(s_q.reshape(B, T, SB_HEADS, SB_HEAD_DIM), k_all, v_all, offset)
    y_sb = (o_sb.reshape(B, T, SB_W) * jax.nn.silu(s_z)) @ w_up_sb

    merged = jax.nn.sigmoid(m_gla) * y_gla + jax.nn.sigmoid(m_sb) * y_sb
    y = x + rms_norm(merged @ w_o, post_gain)
    return y, k_new, v_new, s_new.astype(x.dtype)


def setup_inputs(seed: int = 0) -> dict:
    key = jax.random.key(seed)
    ks = jax.random.split(key, 16)
    f32 = jnp.float32
    nrm = lambda k, shape, scale: jax.random.normal(k, shape, f32) * scale
    return {
        "x_prompt": nrm(ks[0], (BATCH, SEQ, D_MODEL), 1.0),
        "x_sample": nrm(ks[1], (DEC_BATCH, DEC_SEQ, D_MODEL), 1.0),
        "cache_sb_k": nrm(ks[2], (DEPTH, DEC_BATCH, PAST_LEN, SB_HEADS, SB_HEAD_DIM), 1.0),
        "cache_sb_v": nrm(ks[3], (DEPTH, DEC_BATCH, PAST_LEN, SB_HEADS, SB_HEAD_DIM), 1.0),
        "state_gla": nrm(ks[4], (DEPTH, DEC_BATCH, GLA_HEADS, GLA_DK, GLA_DV), 1.0),
        "meta_tokens": nrm(ks[5], (N_META, D_MODEL), 1.0),
        "pre_gain": 1.0 + nrm(ks[6], (DEPTH, D_MODEL), 0.01),
        "w_in": nrm(ks[7], (DEPTH, D_MODEL, IN_COLS), D_MODEL ** -0.5),
        "w_a2": nrm(ks[8], (DEPTH, GLA_RANK, GLA_QK), GLA_RANK ** -0.5),
        "b_a": nrm(ks[9], (DEPTH, GLA_QK), 0.01),
        "gla_gain": 1.0 + nrm(ks[10], (DEPTH, GLA_HEADS, GLA_DV), 0.01),
        "w_up_gla": nrm(ks[11], (DEPTH, GLA_V, D_MODEL), GLA_V ** -0.5),
        "w_up_sb": nrm(ks[12], (DEPTH, SB_W, D_MODEL), SB_W ** -0.5),
        "w_o": nrm(ks[13], (DEPTH, D_MODEL, D_MODEL), D_MODEL ** -0.5),
        "post_gain": 1.0 + nrm(ks[14], (DEPTH, D_MODEL), 0.01),
    }


def reference(x_prompt, x_sample, cache_sb_k, cache_sb_v, state_gla, meta_tokens, pre_gain, w_in, w_a2,
              b_a, gla_gain, w_up_gla, w_up_sb, w_o, post_gain):
    B = x_prompt.shape[0]
    meta = jnp.broadcast_to(meta_tokens.astype(x_prompt.dtype)[None], (B, N_META, D_MODEL))
    hp = jnp.concatenate([meta, x_prompt], axis=1)
    hs = x_sample
    zero_state = jnp.zeros((B, GLA_HEADS, GLA_DK, GLA_DV), jnp.float32)
    kp, vp, sp, ks_, vs_, ss_ = [], [], [], [], [], []
    for l in range(DEPTH):
        hp, k_, v_, s_ = mixer_layer(hp, None, None, zero_state, pre_gain[l], w_in[l], w_a2[l], b_a[l],
                                     gla_gain[l], w_up_gla[l], w_up_sb[l], w_o[l], post_gain[l])
        kp.append(k_); vp.append(v_); sp.append(s_)
        hs, k_, v_, s_ = mixer_layer(hs, cache_sb_k[l], cache_sb_v[l], state_gla[l], pre_gain[l], w_in[l],
                                     w_a2[l], b_a[l], gla_gain[l], w_up_gla[l], w_up_sb[l], w_o[l],
                                     post_gain[l])
        ks_.append(k_); vs_.append(v_); ss_.append(s_)
    return (hp[:, N_META:], hs, jnp.stack(kp), jnp.stack(vp), jnp.stack(sp),
            jnp.stack(ks_), jnp.stack(vs_), jnp.stack(ss_))
```

```python
import functools

import jax
import jax.numpy as jnp
from jax import lax
from jax.experimental import pallas as pl
from jax.experimental.pallas import tpu as pltpu

F32 = jnp.float32
BF16 = jnp.bfloat16

D_MODEL = 2048
N_META = 16
EPS = 1e-6
GLA_HEADS = 4
GLA_DK = 128
GLA_DV = 256
GLA_RANK = 16
GLA_TAU = 16.0
SB_HEADS = 8
SB_HEAD_DIM = 128
GLA_QK = GLA_HEADS * GLA_DK
GLA_V = GLA_HEADS * GLA_DV
SB_W = SB_HEADS * SB_HEAD_DIM

LANES = 128
PROJ_TILE = 1024
P_COLS = 11 * PROJ_TILE
COL_SK, COL_SV, COL_SQ, COL_SZ, COL_GV, COL_GZ, COL_MG, COL_MS, COL_GQ, COL_GK = (
    0, 1024, 2048, 3072, 4096, 5120, 6144, 8192, 10240, 10752)

GLA_CHUNK = 64
GLA_SAFE_LOG_DECAY = -1.0
SB_BLOCK = 128
VMEM_LIMIT = 56 * 1024 * 1024


def _row_tile(n, target):
    best = None
    for d in range(16, min(n, target) + 1, 16):
        if n % d == 0:
            best = d
    return best if best is not None else n


def _sigmoid(x):
    return 1.0 / (1.0 + jnp.exp(-x))


def _log_sigmoid(x):
    return jnp.minimum(x, 0.0) - jnp.log(1.0 + jnp.exp(-jnp.abs(x)))


def _dot(a, b):
    return jnp.dot(a, b, preferred_element_type=F32)


def _dot_nt(a, b):
    return lax.dot_general(a, b, (((1,), (1,)), ((), ())), preferred_element_type=F32)


def _dot_tn(a, b):
    return lax.dot_general(a, b, (((0,), (0,)), ((), ())), preferred_element_type=F32)


def _split3(x):
    hi = x.astype(BF16)
    r = x - hi.astype(F32)
    mid = r.astype(BF16)
    lo = (r - mid.astype(F32)).astype(BF16)
    return hi, mid, lo


def _in_proj_kernel(x_ref, g_ref, w_ref, wr_ref, wa2_ref, ba_ref,
                    p_ref, k_ref, v_ref, la_ref, lamin_ref, h_ref, *, sub):
    j = pl.program_id(1)
    tm = x_ref.shape[0]

    @pl.when(j == 0)
    def _():
        gain = g_ref[...]

        def body(r, carry):
            rows = pl.ds(pl.multiple_of(r * sub, sub), sub)
            x = x_ref[rows, :]
            ms = jnp.mean(x * x, axis=-1, keepdims=True)
            h_ref[rows, :] = (x * lax.rsqrt(ms + EPS) * gain).astype(BF16)
            return carry

        lax.fori_loop(0, tm // sub, body, 0)
        g_r = _dot(h_ref[...], wr_ref[...])
        hi, mid, lo = _split3(g_r)
        w2 = wa2_ref[...]
        w2h, w2m, w2l = _split3(w2)
        xa = (_dot(hi, w2h) + _dot(hi, w2m) + _dot(mid, w2h)
              + _dot(hi, w2l) + _dot(mid, w2m) + _dot(lo, w2h)) + ba_ref[...]
        la = _log_sigmoid(xa) * (1.0 / GLA_TAU)
        la_ref[...] = la
        m = jnp.min(jnp.min(la, axis=0, keepdims=True), axis=1, keepdims=True)
        lamin_ref[...] = jnp.broadcast_to(m.reshape(1, 1, 1), lamin_ref.shape)

    acc = _dot(h_ref[...], w_ref[...])
    p_ref[...] = acc.astype(BF16)

    @pl.when(j == 0)
    def _():
        k_ref[...] = acc

    @pl.when(j == 1)
    def _():
        v_ref[...] = acc


def _in_proj(x, gain, w_main, w_r, w_a2p, b_a, *, row_target):
    n = x.shape[0]
    tm = _row_tile(n, row_target)
    nt = n // tm
    sub = _row_tile(tm, 64)
    grid = (nt, P_COLS // PROJ_TILE)
    const = lambda i, j: (0, 0)
    return pl.pallas_call(
        functools.partial(_in_proj_kernel, sub=sub),
        grid=grid,
        in_specs=[
            pl.BlockSpec((tm, D_MODEL), lambda i, j: (i, 0)),
            pl.BlockSpec((1, D_MODEL), const),
            pl.BlockSpec((D_MODEL, PROJ_TILE), lambda i, j: (0, j)),
            pl.BlockSpec((D_MODEL, LANES), const),
            pl.BlockSpec((LANES, GLA_QK), const),
            pl.BlockSpec((1, GLA_QK), const),
        ],
        out_specs=[
            pl.BlockSpec((tm, PROJ_TILE), lambda i, j: (i, j)),
            pl.BlockSpec((tm, SB_W), lambda i, j: (i, 0)),
            pl.BlockSpec((tm, SB_W), lambda i, j: (i, 0)),
            pl.BlockSpec((tm, GLA_QK), lambda i, j: (i, 0)),
            pl.BlockSpec((1, 8, LANES), lambda i, j: (i, 0, 0)),
        ],
        out_shape=[
            jax.ShapeDtypeStruct((n, P_COLS), BF16),
            jax.ShapeDtypeStruct((n, SB_W), F32),
            jax.ShapeDtypeStruct((n, SB_W), F32),
            jax.ShapeDtypeStruct((n, GLA_QK), F32),
            jax.ShapeDtypeStruct((nt, 8, LANES), F32),
        ],
        scratch_shapes=[pltpu.VMEM((tm, D_MODEL), BF16)],
        compiler_params=pltpu.CompilerParams(
            dimension_semantics=("arbitrary", "arbitrary"), vmem_limit_bytes=VMEM_LIMIT),
    )(x, gain, w_main, w_r, w_a2p, b_a)


def _gla_kernel(safe_ref, q_ref, k_ref, v_ref, z_ref, la_ref, s0_ref, gain_ref,
                o_ref, sout_ref, st_ref, oi_ref, qs_ref, bs_ref, *, chunk):
    C = chunk
    T = q_ref.shape[0]
    n_full, rem = T // C, T % C
    safe = safe_ref[0] > 0

    row = lax.broadcasted_iota(jnp.int32, (C, C), 0)
    col = lax.broadcasted_iota(jnp.int32, (C, C), 1)
    causal = row >= col
    tri = jnp.where(causal, 1.0, 0.0).astype(BF16)
    row1 = lax.broadcasted_iota(jnp.int32, (C, 1), 0)
    gain = gain_ref[...]

    st_ref[...] = s0_ref[...].T

    def step(off, pad_front):
        rows = pl.ds(off, C)
        q = q_ref[rows, :].astype(F32) * (GLA_DK ** -0.5)
        k = k_ref[rows, :].astype(F32)
        v = v_ref[rows, :]
        la = la_ref[rows, :]
        if pad_front:
            valid = row1 >= pad_front
            la = jnp.where(valid, la, 0.0)
            k = jnp.where(valid, k, 0.0)
            v = jnp.where(valid, v, jnp.zeros_like(v))
        hi, mid, lo = _split3(la)
        b = _dot(tri, hi) + _dot(tri, mid) + _dot(tri, lo)
        b_end = b[C - 1:C, :]
        qe = (q * jnp.exp(b)).astype(BF16)
        kd = (k * jnp.exp(b_end - b)).astype(BF16)
        st = st_ref[...]
        o_inter = _dot_nt(qe, st.astype(BF16))

        @pl.when(safe)
        def _():
            ke = (k * jnp.exp(-b)).astype(BF16)
            s = jnp.where(causal, _dot_nt(qe, ke), 0.0).astype(BF16)
            oi_ref[...] = _dot(s, v)

        @pl.when(jnp.logical_not(safe))
        def _():
            qs_ref[...] = q
            bs_ref[...] = b
            vf = v.astype(F32)

            def body(t, carry):
                bt = bs_ref[pl.ds(t, 1), :]
                qt = qs_ref[pl.ds(t, 1), :]
                diff = jnp.where(row1 <= t, bt - b, -jnp.inf)
                w = jnp.sum(k * jnp.exp(diff) * qt, axis=-1, keepdims=True)
                oi_ref[pl.ds(t, 1), :] = jnp.sum(w * vf, axis=0, keepdims=True)
                return carry

            lax.fori_loop(0, C, body, 0)

        o = o_inter + oi_ref[...]
        st_ref[...] = st * jnp.exp(b_end) + _dot_tn(v, kd)

        on = o * lax.rsqrt(jnp.mean(o * o, axis=-1, keepdims=True) + EPS) * gain
        z = z_ref[rows, :].astype(F32)
        out = (on * (z * _sigmoid(z))).astype(BF16)
        if pad_front:
            o_ref[pl.ds(off + pad_front, C - pad_front), :] = out[pad_front:, :]
        else:
            o_ref[rows, :] = out

    def body(c, carry):
        step(pl.multiple_of(c * C, C), 0)
        return carry

    lax.fori_loop(0, n_full, body, 0)
    if rem:
        step(T - C, C - rem)
    sout_ref[...] = st_ref[...].T


def _gla(p3, la3, s0, gla_gain, safe, *, chunk=GLA_CHUNK):
    B, T, _ = p3.shape
    H = GLA_HEADS
    kq, kk = COL_GQ // GLA_DK, COL_GK // GLA_DK
    kv, kz = COL_GV // GLA_DV, COL_GZ // GLA_DV
    grid_spec = pltpu.PrefetchScalarGridSpec(
        num_scalar_prefetch=1,
        grid=(B, H),
        in_specs=[
            pl.BlockSpec((None, T, GLA_DK), lambda b, h, s: (b, 0, kq + h)),
            pl.BlockSpec((None, T, GLA_DK), lambda b, h, s: (b, 0, kk + h)),
            pl.BlockSpec((None, T, GLA_DV), lambda b, h, s: (b, 0, kv + h)),
            pl.BlockSpec((None, T, GLA_DV), lambda b, h, s: (b, 0, kz + h)),
            pl.BlockSpec((None, T, GLA_DK), lambda b, h, s: (b, 0, h)),
            pl.BlockSpec((None, None, GLA_DK, GLA_DV), lambda b, h, s: (b, h, 0, 0)),
            pl.BlockSpec((None, 1, GLA_DV), lambda b, h, s: (h, 0, 0)),
        ],
        out_specs=[
            pl.BlockSpec((None, T, GLA_DV), lambda b, h, s: (b, 0, h)),
            pl.BlockSpec((None, None, GLA_DK, GLA_DV), lambda b, h, s: (b, h, 0, 0)),
        ],
        scratch_shapes=[
            pltpu.VMEM((GLA_DV, GLA_DK), F32),
            pltpu.VMEM((chunk, GLA_DV), F32),
            pltpu.VMEM((chunk, GLA_DK), F32),
            pltpu.VMEM((chunk, GLA_DK), F32),
        ],
    )
    return pl.pallas_call(
        functools.partial(_gla_kernel, chunk=chunk),
        grid_spec=grid_spec,
        out_shape=[
            jax.ShapeDtypeStruct((B, T, GLA_V), BF16),
            jax.ShapeDtypeStruct((B, H, GLA_DK, GLA_DV), F32),
        ],
        compiler_params=pltpu.CompilerParams(
            dimension_semantics=("arbitrary", "arbitrary"), vmem_limit_bytes=VMEM_LIMIT),
    )(safe, p3, p3, p3, p3, la3, s0, gla_gain.reshape(H, 1, GLA_DV))


def _strict_upper(n):
    j = lax.broadcasted_iota(jnp.int32, (n, n), 0)
    s = lax.broadcasted_iota(jnp.int32, (n, n), 1)
    return jnp.where(j > s, 1.0, 0.0).astype(BF16)


def _sb_step(q, kb, vb, upper, mask, carry, acc):
    z = _dot_nt(q, kb)
    lb = _log_sigmoid(z)
    lom = lb - z
    if mask is not None:
        lom = jnp.where(mask, lom, 0.0)
    hi = lom.astype(BF16)
    lo = (lom - hi.astype(F32)).astype(BF16)
    tail = _dot(hi, upper) + _dot(lo, upper) + carry
    w = jnp.exp(lb + tail)
    if mask is not None:
        w = jnp.where(mask, w, 0.0)
    acc = acc + _dot(w.astype(BF16), vb)
    carry = carry + jnp.sum(lom, axis=-1, keepdims=True)
    return carry, acc


def _sb_prompt_kernel(q_ref, k_ref, v_ref, z_ref, o_ref, *, blk):
    T = q_ref.shape[0]
    n_full, rem = T // blk, T % blk
    upper = _strict_upper(blk)
    row = lax.broadcasted_iota(jnp.int32, (blk, blk), 0)
    col = lax.broadcasted_iota(jnp.int32, (blk, blk), 1)
    strict = col < row
    scale = SB_HEAD_DIM ** -0.5

    def q_block(r0, n_prev, head_cols):
        rows = pl.ds(r0, blk)
        q = (q_ref[rows, :].astype(F32) * scale).astype(BF16)
        carry = jnp.zeros((blk, 1), F32)
        acc = jnp.zeros((blk, SB_HEAD_DIM), F32)
        carry, acc = _sb_step(q, k_ref[rows, :], v_ref[rows, :], upper, strict, carry, acc)

        def prev(n, c):
            s0 = pl.multiple_of(r0 - (n + 1) * blk, 16)
            ks = pl.ds(s0, blk)
            return _sb_step(q, k_ref[ks, :], v_ref[ks, :], upper, None, *c)

        carry, acc = lax.fori_loop(0, n_prev, prev, (carry, acc))
        if head_cols:
            carry, acc = _sb_step(q, k_ref[0:blk, :], v_ref[0:blk, :], upper, col < head_cols, carry, acc)
        zg = z_ref[rows, :].astype(F32)
        o_ref[rows, :] = (acc * (zg * _sigmoid(zg))).astype(BF16)

    def body(i, carry):
        q_block(pl.multiple_of(i * blk, blk), i, 0)
        return carry

    lax.fori_loop(0, n_full, body, 0)
    if rem:
        q_block(T - blk, n_full - 1, rem)


def _sb_prompt(p3, *, blk=SB_BLOCK):
    B, T, _ = p3.shape
    d = SB_HEAD_DIM
    spec = lambda c: pl.BlockSpec((None, T, d), lambda b, h: (b, 0, c // d + h))
    return pl.pallas_call(
        functools.partial(_sb_prompt_kernel, blk=blk),
        grid=(B, SB_HEADS),
        in_specs=[spec(COL_SQ), spec(COL_SK), spec(COL_SV), spec(COL_SZ)],
        out_specs=pl.BlockSpec((None, T, d), lambda b, h: (b, 0, h)),
        out_shape=jax.ShapeDtypeStruct((B, T, SB_W), BF16),
        compiler_params=pltpu.CompilerParams(
            dimension_semantics=("arbitrary", "arbitrary"), vmem_limit_bytes=VMEM_LIMIT),
    )(p3, p3, p3, p3)


def _sb_sample_kernel(q_ref, k_ref, v_ref, z_ref, pk_ref, pv_ref, o_ref, *, blk):
    tq = q_ref.shape[0]
    n_past = pk_ref.shape[0] // blk
    past_len = pk_ref.shape[0]
    row = lax.broadcasted_iota(jnp.int32, (tq, tq), 0)
    col = lax.broadcasted_iota(jnp.int32, (tq, tq), 1)
    scale = SB_HEAD_DIM ** -0.5
    q = (q_ref[...].astype(F32) * scale).astype(BF16)
    carry = jnp.zeros((tq, 1), F32)
    acc = jnp.zeros((tq, SB_HEAD_DIM), F32)
    carry, acc = _sb_step(q, k_ref[...], v_ref[...], _strict_upper(tq), col < row, carry, acc)
    upper = _strict_upper(blk)

    def prev(n, c):
        s0 = pl.multiple_of(past_len - (n + 1) * blk, blk)
        ks = pl.ds(s0, blk)
        return _sb_step(q, pk_ref[ks, :].astype(BF16), pv_ref[ks, :].astype(BF16), upper, None, *c)

    carry, acc = lax.fori_loop(0, n_past, prev, (carry, acc))
    zg = z_ref[...].astype(F32)
    o_ref[...] = (acc * (zg * _sigmoid(zg))).astype(BF16)


def _sb_sample(p3, past_k, past_v, *, blk=SB_BLOCK):
    B, T, _ = p3.shape
    d = SB_HEAD_DIM
    past_len = past_k.shape[1]
    assert past_len % blk == 0 and T % 16 == 0
    spec = lambda c: pl.BlockSpec((None, T, d), lambda b, h: (b, 0, c // d + h))
    past = pl.BlockSpec((None, past_len, d), lambda b, h: (b, 0, h))
    return pl.pallas_call(
        functools.partial(_sb_sample_kernel, blk=blk),
        grid=(B, SB_HEADS),
        in_specs=[spec(COL_SQ), spec(COL_SK), spec(COL_SV), spec(COL_SZ), past, past],
        out_specs=pl.BlockSpec((None, T, d), lambda b, h: (b, 0, h)),
        out_shape=jax.ShapeDtypeStruct((B, T, SB_W), BF16),
        compiler_params=pltpu.CompilerParams(
            dimension_semantics=("arbitrary", "arbitrary"), vmem_limit_bytes=VMEM_LIMIT),
    )(p3, p3, p3, p3, past_k, past_v)


def _out_proj_kernel(og_ref, os_ref, mg_ref, ms_ref, x_ref, wg_ref, ws_ref, wo_ref, g_ref, y_ref):
    y_gla = _dot(og_ref[...], wg_ref[...])
    y_sb = _dot(os_ref[...], ws_ref[...])
    merged = (_sigmoid(mg_ref[...].astype(F32)) * y_gla + _sigmoid(ms_ref[...].astype(F32)) * y_sb)
    u = _dot(merged.astype(BF16), wo_ref[...])
    un = u * lax.rsqrt(jnp.mean(u * u, axis=-1, keepdims=True) + EPS) * g_ref[...]
    y_ref[...] = x_ref[...] + un


def _out_proj(og, osb, p, x, w_up_gla, w_up_sb, w_o, post_gain, *, row_target):
    n = x.shape[0]
    tm = _row_tile(n, row_target)
    const = lambda i: (0, 0)
    cm = D_MODEL
    return pl.pallas_call(
        _out_proj_kernel,
        grid=(n // tm,),
        in_specs=[
            pl.BlockSpec((tm, GLA_V), lambda i: (i, 0)),
            pl.BlockSpec((tm, SB_W), lambda i: (i, 0)),
            pl.BlockSpec((tm, cm), lambda i: (i, COL_MG // cm)),
            pl.BlockSpec((tm, cm), lambda i: (i, COL_MS // cm)),
            pl.BlockSpec((tm, cm), lambda i: (i, 0)),
            pl.BlockSpec((GLA_V, cm), const),
            pl.BlockSpec((SB_W, cm), const),
            pl.BlockSpec((cm, cm), const),
            pl.BlockSpec((1, cm), const),
        ],
        out_specs=pl.BlockSpec((tm, cm), lambda i: (i, 0)),
        out_shape=jax.ShapeDtypeStruct((n, cm), F32),
        compiler_params=pltpu.CompilerParams(
            dimension_semantics=("arbitrary",), vmem_limit_bytes=VMEM_LIMIT),
    )(og, osb, p, p, x, w_up_gla, w_up_sb, w_o, post_gain)


def _pack_weights(pre_gain, w_in, w_a2, b_a, gla_gain, w_up_gla, w_up_sb, w_o, post_gain):
    sizes = (GLA_QK, GLA_QK, GLA_V, GLA_V, GLA_RANK, SB_W, SB_W, SB_W, SB_W, D_MODEL, D_MODEL)
    offs = [0]
    for s in sizes:
        offs.append(offs[-1] + s)
    part = lambda i: w_in[:, offs[i]:offs[i + 1]]
    g_q, g_k, g_v, g_z, g_r, s_q, s_k, s_v, s_z, m_gla, m_sb = (part(i) for i in range(11))
    w_main = jnp.concatenate([s_k, s_v, s_q, s_z, g_v, g_z, m_gla, m_sb, g_q, g_k], axis=1).astype(BF16)
    w_r = jnp.pad(g_r, ((0, 0), (0, LANES - GLA_RANK))).astype(BF16)
    w_a2p = jnp.pad(w_a2.astype(F32), ((0, LANES - GLA_RANK), (0, 0)))
    return dict(
        pre_gain=pre_gain.reshape(1, D_MODEL).astype(F32), w_main=w_main, w_r=w_r, w_a2p=w_a2p,
        b_a=b_a.reshape(1, GLA_QK).astype(F32), gla_gain=gla_gain.astype(F32),
        w_up_gla=w_up_gla.astype(BF16), w_up_sb=w_up_sb.astype(BF16), w_o=w_o.astype(BF16),
        post_gain=post_gain.reshape(1, D_MODEL).astype(F32))


def _layer(x, batch, past_k, past_v, s0, w, *, in_rows, out_rows):
    n = x.shape[0]
    T = n // batch
    p, k_new, v_new, la, la_min = _in_proj(
        x, w["pre_gain"], w["w_main"], w["w_r"], w["w_a2p"], w["b_a"], row_target=in_rows)
    safe = (jnp.min(la_min) >= GLA_SAFE_LOG_DECAY).astype(jnp.int32).reshape(1)
    p3 = p.reshape(batch, T, P_COLS)
    og, s_new = _gla(p3, la.reshape(batch, T, GLA_QK), s0, w["gla_gain"], safe)
    if past_k is None:
        osb = _sb_prompt(p3)
    else:
        osb = _sb_sample(p3, past_k, past_v)
    y = _out_proj(og.reshape(n, GLA_V), osb.reshape(n, SB_W), p, x,
                  w["w_up_gla"], w["w_up_sb"], w["w_o"], w["post_gain"], row_target=out_rows)
    return y, k_new, v_new, s_new


def kernel(x_prompt, x_sample, cache_sb_k, cache_sb_v, state_gla, meta_tokens, pre_gain, w_in, w_a2, b_a,
           gla_gain, w_up_gla, w_up_sb, w_o, post_gain):
    B, seq, _ = x_prompt.shape
    Bs, seq_s, _ = x_sample.shape
    depth = w_in.shape[0]
    T = N_META + seq
    meta = jnp.broadcast_to(meta_tokens.astype(x_prompt.dtype)[None], (B, N_META, D_MODEL))
    hp = jnp.concatenate([meta, x_prompt], axis=1).reshape(B * T, D_MODEL)
    hs = x_sample.reshape(Bs * seq_s, D_MODEL)
    zero_state = jnp.zeros((B, GLA_HEADS, GLA_DK, GLA_DV), F32)
    past_len = cache_sb_k.shape[2]
    kp, vp, sp, ks_, vs_, ss_ = [], [], [], [], [], []
    for l in range(depth):
        w = _pack_weights(pre_gain[l], w_in[l], w_a2[l], b_a[l], gla_gain[l], w_up_gla[l], w_up_sb[l],
                          w_o[l], post_gain[l])
        hp, k_, v_, s_ = _layer(hp, B, None, None, zero_state, w, in_rows=768, out_rows=256)
        kp.append(k_.reshape(B, T, SB_HEADS, SB_HEAD_DIM))
        vp.append(v_.reshape(B, T, SB_HEADS, SB_HEAD_DIM))
        sp.append(s_)
        hs, k_, v_, s_ = _layer(hs, Bs, cache_sb_k[l].reshape(Bs, past_len, SB_W),
                                cache_sb_v[l].reshape(Bs, past_len, SB_W), state_gla[l], w,
                                in_rows=512, out_rows=256)
        ks_.append(k_.reshape(Bs, seq_s, SB_HEADS, SB_HEAD_DIM))
        vs_.append(v_.reshape(Bs, seq_s, SB_HEADS, SB_HEAD_DIM))
        ss_.append(s_)
    y_prompt = hp.reshape(B, T, D_MODEL)[:, N_META:]
    y_sample = hs.reshape(Bs, seq_s, D_MODEL)
    return (y_prompt, y_sample, jnp.stack(kp), jnp.stack(vp), jnp.stack(sp),
            jnp.stack(ks_), jnp.stack(vs_), jnp.stack(ss_))
```

```python
import functools

import jax
import jax.numpy as jnp
from jax import lax
from jax.experimental import pallas as pl
from jax.experimental.pallas import tpu as pltpu

F32 = jnp.float32
BF16 = jnp.bfloat16

D_MODEL = 2048
N_META = 16
EPS = 1e-6
GLA_HEADS = 4
GLA_DK = 128
GLA_DV = 256
GLA_RANK = 16
GLA_TAU = 16.0
SB_HEADS = 8
SB_HEAD_DIM = 128
GLA_QK = GLA_HEADS * GLA_DK
GLA_V = GLA_HEADS * GLA_DV
SB_W = SB_HEADS * SB_HEAD_DIM

LANES = 128
PROJ_TILE = 1024
P_COLS = 11 * PROJ_TILE
COL_SK, COL_SV, COL_SQ, COL_SZ, COL_GV, COL_GZ, COL_MG, COL_MS, COL_GQ, COL_GK = (
    0, 1024, 2048, 3072, 4096, 5120, 6144, 8192, 10240, 10752)

GLA_CHUNK = 64
GLA_HEADS_PER_STEP = 2
GLA_SAFE_LOG_DECAY = -1.0
SB_BLOCK = 128
SB_HEADS_PER_STEP = 4
SB_STATIC_BLOCKS = 2
SB_STOP_LOG = -120.0
VMEM_LIMIT = 56 * 1024 * 1024


def _row_tile(n, target):
    best = None
    for d in range(16, min(n, target) + 1, 16):
        if n % d == 0:
            best = d
    return best if best is not None else n


def _sigmoid(x):
    return 1.0 / (1.0 + jnp.exp(-x))


def _log_sigmoid(x):
    return jnp.minimum(x, 0.0) - jnp.log(1.0 + jnp.exp(-jnp.abs(x)))


def _dot(a, b):
    return jnp.dot(a, b, preferred_element_type=F32)


def _dot_nt(a, b):
    return lax.dot_general(a, b, (((1,), (1,)), ((), ())), preferred_element_type=F32)


def _dot_tn(a, b):
    return lax.dot_general(a, b, (((0,), (0,)), ((), ())), preferred_element_type=F32)


def _split3(x):
    hi = x.astype(BF16)
    r = x - hi.astype(F32)
    mid = r.astype(BF16)
    lo = (r - mid.astype(F32)).astype(BF16)
    return hi, mid, lo


def _in_proj_kernel(x_ref, g_ref, w_ref, wr_ref, wa2_ref, ba_ref, *rest, sub, aliased):
    if aliased:
        rest = rest[2:]
    p_ref, k_ref, v_ref, la_ref, lamin_ref, h_ref = rest
    j = pl.program_id(1)
    tm = x_ref.shape[0]

    @pl.when(j == 0)
    def _():
        gain = g_ref[...]

        def body(r, carry):
            rows = pl.ds(pl.multiple_of(r * sub, sub), sub)
            x = x_ref[rows, :]
            ms = jnp.mean(x * x, axis=-1, keepdims=True)
            h_ref[rows, :] = (x * lax.rsqrt(ms + EPS) * gain).astype(BF16)
            return carry

        lax.fori_loop(0, tm // sub, body, 0)
        g_r = _dot(h_ref[...], wr_ref[...])
        hi, mid, lo = _split3(g_r)
        w2 = wa2_ref[...]
        w2h, w2m, w2l = _split3(w2)
        xa = (_dot(hi, w2h) + _dot(hi, w2m) + _dot(mid, w2h)
              + _dot(hi, w2l) + _dot(mid, w2m) + _dot(lo, w2h)) + ba_ref[...]
        la = _log_sigmoid(xa) * (1.0 / GLA_TAU)
        la_ref[...] = la
        m = jnp.min(jnp.min(la, axis=0, keepdims=True), axis=1, keepdims=True)
        lamin_ref[...] = jnp.broadcast_to(m.reshape(1, 1, 1), lamin_ref.shape)

    acc = _dot(h_ref[...], w_ref[...])
    p_ref[...] = acc.astype(BF16)

    def heads_out(ref):
        for h in range(SB_HEADS):
            ref[:, h, :] = acc[:, h * SB_HEAD_DIM:(h + 1) * SB_HEAD_DIM]

    @pl.when(j == COL_SK // PROJ_TILE)
    def _():
        heads_out(k_ref)

    @pl.when(j == COL_SV // PROJ_TILE)
    def _():
        heads_out(v_ref)


def _in_proj(x, gain, w_main, w_r, w_a2p, b_a, kv_prev, *, layer, depth, row_target):
    n = x.shape[0]
    tm = _row_tile(n, row_target)
    nt = n // tm
    sub = _row_tile(tm, 64)
    grid = (nt, P_COLS // PROJ_TILE)
    const = lambda i, j: (0, 0)
    aliased = kv_prev is not None
    kv_spec = pl.BlockSpec((None, tm, SB_HEADS, SB_HEAD_DIM), lambda i, j: (layer, i, 0, 0))
    kv_shape = jax.ShapeDtypeStruct((depth, n, SB_HEADS, SB_HEAD_DIM), F32)
    in_specs = [
        pl.BlockSpec((tm, D_MODEL), lambda i, j: (i, 0)),
        pl.BlockSpec((1, D_MODEL), const),
        pl.BlockSpec((D_MODEL, PROJ_TILE), lambda i, j: (0, j)),
        pl.BlockSpec((D_MODEL, LANES), const),
        pl.BlockSpec((LANES, GLA_QK), const),
        pl.BlockSpec((1, GLA_QK), const),
    ]
    args = [x, gain, w_main, w_r, w_a2p, b_a]
    aliases = {}
    if aliased:
        in_specs += [pl.BlockSpec(memory_space=pl.ANY), pl.BlockSpec(memory_space=pl.ANY)]
        args += list(kv_prev)
        aliases = {6: 1, 7: 2}
    return pl.pallas_call(
        functools.partial(_in_proj_kernel, sub=sub, aliased=aliased),
        grid=grid,
        in_specs=in_specs,
        out_specs=[
            pl.BlockSpec((tm, PROJ_TILE), lambda i, j: (i, j)),
            kv_spec,
            kv_spec,
            pl.BlockSpec((tm, GLA_QK), lambda i, j: (i, 0)),
            pl.BlockSpec((1, 8, LANES), lambda i, j: (i, 0, 0)),
        ],
        out_shape=[
            jax.ShapeDtypeStruct((n, P_COLS), BF16),
            kv_shape,
            kv_shape,
            jax.ShapeDtypeStruct((n, GLA_QK), F32),
            jax.ShapeDtypeStruct((nt, 8, LANES), F32),
        ],
        scratch_shapes=[pltpu.VMEM((tm, D_MODEL), BF16)],
        input_output_aliases=aliases,
        compiler_params=pltpu.CompilerParams(
            dimension_semantics=("arbitrary", "arbitrary"), vmem_limit_bytes=VMEM_LIMIT),
    )(*args)


def _gla_kernel(safe_ref, q_ref, k_ref, v_ref, z_ref, la_ref, s0_ref, gain_ref,
                o_ref, sout_ref, st_ref, oi_ref, qs_ref, bs_ref, *, chunk, hp):
    C = chunk
    T = q_ref.shape[0]
    n_full, rem = T // C, T % C
    safe = safe_ref[0] > 0

    row = lax.broadcasted_iota(jnp.int32, (C, C), 0)
    col = lax.broadcasted_iota(jnp.int32, (C, C), 1)
    causal = row >= col
    tri = jnp.where(causal, 1.0, 0.0).astype(BF16)
    tri3 = jnp.concatenate([tri, tri, tri], axis=1)
    row1 = lax.broadcasted_iota(jnp.int32, (C, 1), 0)

    for hh in range(hp):
        st_ref[hh] = s0_ref[hh].T

    def group(chunks, fast):
        items = [(hh, off, pad) for off, pad in chunks for hh in range(hp)]
        kcols = lambda hh: slice(hh * GLA_DK, (hh + 1) * GLA_DK)
        vcols = lambda hh: slice(hh * GLA_DV, (hh + 1) * GLA_DV)
        qs, ks, vs, las = [], [], [], []
        for hh, off, pad in items:
            rows = pl.ds(off, C)
            q = q_ref[rows, kcols(hh)].astype(F32) * (GLA_DK ** -0.5)
            k = k_ref[rows, kcols(hh)].astype(F32)
            v = v_ref[rows, vcols(hh)]
            la = la_ref[rows, kcols(hh)]
            if pad:
                valid = row1 >= pad
                la = jnp.where(valid, la, 0.0)
                k = jnp.where(valid, k, 0.0)
                v = jnp.where(valid, v, jnp.zeros_like(v))
            qs.append(q)
            ks.append(k)
            vs.append(v)
            las.append(jnp.concatenate(_split3(la), axis=0))
        bs = [_dot(tri3, la3) for la3 in las]
        qes, kds, ends = [], [], []
        for q, k, b in zip(qs, ks, bs):
            b_end = b[C - 1:C, :]
            qes.append((q * jnp.exp(b)).astype(BF16))
            kds.append((k * jnp.exp(b_end - b)).astype(BF16))
            ends.append(jnp.exp(b_end))
        if fast:
            kes = [(k * jnp.exp(-b)).astype(BF16) for k, b in zip(ks, bs)]
            ss = [jnp.where(causal, _dot_nt(qe, ke), 0.0).astype(BF16) for qe, ke in zip(qes, kes)]
            intra = [_dot(s, v) for s, v in zip(ss, vs)]
        else:
            intra = []
            for q, k, v, b in zip(qs, ks, vs, bs):
                qs_ref[...] = q
                bs_ref[...] = b
                vf = v.astype(F32)

                def body(t, carry, k=k, b=b, vf=vf):
                    bt = bs_ref[pl.ds(t, 1), :]
                    qt = qs_ref[pl.ds(t, 1), :]
                    diff = jnp.where(row1 <= t, bt - b, -jnp.inf)
                    w = jnp.sum(k * jnp.exp(diff) * qt, axis=-1, keepdims=True)
                    oi_ref[pl.ds(t, 1), :] = jnp.sum(w * vf, axis=0, keepdims=True)
                    return carry

                lax.fori_loop(0, C, body, 0)
                intra.append(oi_ref[...])
        kvs = [_dot_tn(v, kd) for v, kd in zip(vs, kds)]
        sts = [st_ref[hh] for hh in range(hp)]
        os_ = []
        for i, (hh, off, pad) in enumerate(items):
            os_.append(_dot_nt(qes[i], sts[hh].astype(BF16)) + intra[i])
            sts[hh] = sts[hh] * ends[i] + kvs[i]
        for hh in range(hp):
            st_ref[hh] = sts[hh]
        for (hh, off, pad), o in zip(items, os_):
            on = o * lax.rsqrt(jnp.mean(o * o, axis=-1, keepdims=True) + EPS) * gain_ref[hh]
            z = z_ref[pl.ds(off, C), vcols(hh)].astype(F32)
            out = (on * (z * _sigmoid(z))).astype(BF16)
            if pad:
                o_ref[pl.ds(off + pad, C - pad), vcols(hh)] = out[pad:, :]
            else:
                o_ref[pl.ds(off, C), vcols(hh)] = out

    def run(fast):
        per = 2 if n_full % 2 == 0 else 1

        def body(c, carry):
            base = pl.multiple_of(c * (per * C), per * C)
            group([(base + i * C, 0) for i in range(per)], fast)
            return carry

        lax.fori_loop(0, n_full // per, body, 0)
        if rem:
            group([(T - C, C - rem)], fast)

    pl.when(safe)(lambda: run(True))
    pl.when(jnp.logical_not(safe))(lambda: run(False))
    for hh in range(hp):
        sout_ref[hh] = st_ref[hh].T


def _gla(p3, la3, s0, gla_gain, safe, *, chunk=GLA_CHUNK, hp=GLA_HEADS_PER_STEP):
    B, T, _ = p3.shape
    H = GLA_HEADS
    wk, wv = hp * GLA_DK, hp * GLA_DV
    kq, kk = COL_GQ // wk, COL_GK // wk
    kv, kz = COL_GV // wv, COL_GZ // wv
    grid_spec = pltpu.PrefetchScalarGridSpec(
        num_scalar_prefetch=1,
        grid=(B, H // hp),
        in_specs=[
            pl.BlockSpec((None, T, wk), lambda b, g, s: (b, 0, kq + g)),
            pl.BlockSpec((None, T, wk), lambda b, g, s: (b, 0, kk + g)),
            pl.BlockSpec((None, T, wv), lambda b, g, s: (b, 0, kv + g)),
            pl.BlockSpec((None, T, wv), lambda b, g, s: (b, 0, kz + g)),
            pl.BlockSpec((None, T, wk), lambda b, g, s: (b, 0, g)),
            pl.BlockSpec((None, hp, GLA_DK, GLA_DV), lambda b, g, s: (b, g, 0, 0)),
            pl.BlockSpec((hp, 1, GLA_DV), lambda b, g, s: (g, 0, 0)),
        ],
        out_specs=[
            pl.BlockSpec((None, T, wv), lambda b, g, s: (b, 0, g)),
            pl.BlockSpec((None, hp, GLA_DK, GLA_DV), lambda b, g, s: (b, g, 0, 0)),
        ],
        scratch_shapes=[
            pltpu.VMEM((hp, GLA_DV, GLA_DK), F32),
            pltpu.VMEM((chunk, GLA_DV), F32),
            pltpu.VMEM((chunk, GLA_DK), F32),
            pltpu.VMEM((chunk, GLA_DK), F32),
        ],
    )
    return pl.pallas_call(
        functools.partial(_gla_kernel, chunk=chunk, hp=hp),
        grid_spec=grid_spec,
        out_shape=[
            jax.ShapeDtypeStruct((B, T, GLA_V), BF16),
            jax.ShapeDtypeStruct((B, H, GLA_DK, GLA_DV), F32),
        ],
        compiler_params=pltpu.CompilerParams(
            dimension_semantics=("arbitrary", "arbitrary"), vmem_limit_bytes=VMEM_LIMIT),
    )(safe, p3, p3, p3, p3, la3, s0, gla_gain.reshape(H, 1, GLA_DV))


def _strict_upper(n):
    j = lax.broadcasted_iota(jnp.int32, (n, n), 0)
    s = lax.broadcasted_iota(jnp.int32, (n, n), 1)
    return jnp.where(j > s, 1.0, 0.0).astype(BF16)


def _sb_step(q, kb, vb, upper, mask, carry, acc):
    z = _dot_nt(q, kb)
    lb = _log_sigmoid(z)
    lom = lb - z
    if mask is not None:
        lom = jnp.where(mask, lom, 0.0)
    hi = lom.astype(BF16)
    lo = (lom - hi.astype(F32)).astype(BF16)
    tail = _dot(hi, upper) + _dot(lo, upper) + carry
    w = jnp.exp(lb + tail)
    if mask is not None:
        w = jnp.where(mask, w, 0.0)
    acc = acc + _dot(w.astype(BF16), vb)
    carry = carry + jnp.sum(lom, axis=-1, keepdims=True)
    return carry, acc


def _sb_prompt_kernel(q_ref, k_ref, v_ref, z_ref, o_ref, c_ref, a_ref, *, blk, hp, n_static):
    T = q_ref.shape[0]
    d = SB_HEAD_DIM
    n_full, rem = T // blk, T % blk
    upper = _strict_upper(blk)
    upper2 = jnp.concatenate([upper, upper], axis=0)
    row = lax.broadcasted_iota(jnp.int32, (blk, blk), 0)
    col = lax.broadcasted_iota(jnp.int32, (blk, blk), 1)
    strict = col < row
    scale = SB_HEAD_DIM ** -0.5

    def q_block(r0, n_prev, n_stat, head_cols):
        rows = pl.ds(r0, blk)
        heads = [slice(hh * d, (hh + 1) * d) for hh in range(hp)]
        nb = n_stat + 1
        k0 = r0 - n_stat * blk
        krows = pl.ds(k0 if isinstance(k0, int) else pl.multiple_of(k0, blk), nb * blk)
        qs = [(q_ref[rows, cols].astype(F32) * scale).astype(BF16) for cols in heads]
        zs = [_dot_nt(q, k_ref[krows, cols]) for q, cols in zip(qs, heads)]
        lbs, hls, sums = [], [], []
        for z in zs:
            lb = _log_sigmoid(z)
            lom = lb - z
            hl, rs = [], []
            for a in range(nb):
                lom_a = lom[:, a * blk:(a + 1) * blk]
                if a == nb - 1:
                    lom_a = jnp.where(strict, lom_a, 0.0)
                hi = lom_a.astype(BF16)
                lo = (lom_a - hi.astype(F32)).astype(BF16)
                hl.append(jnp.concatenate([hi, lo], axis=1))
                rs.append(jnp.sum(lom_a, axis=-1, keepdims=True))
            lbs.append(lb)
            hls.append(hl)
            sums.append(rs)
        tails = [[_dot(hl_a, upper2) for hl_a in hl] for hl in hls]
        ws, carries = [], []
        for lb, tl, rs in zip(lbs, tails, sums):
            carry = jnp.zeros((blk, 1), F32)
            w = [None] * nb
            for a in range(nb - 1, -1, -1):
                w_a = jnp.exp(lb[:, a * blk:(a + 1) * blk] + tl[a] + carry)
                if a == nb - 1:
                    w_a = jnp.where(strict, w_a, 0.0)
                w[a] = w_a.astype(BF16)
                carry = carry + rs[a]
            ws.append(jnp.concatenate(w, axis=1) if nb > 1 else w[0])
            carries.append(carry)
        accs = [_dot(w, v_ref[krows, cols]) for w, cols in zip(ws, heads)]
        if head_cols or not (isinstance(n_prev, int) and n_prev == n_stat):
            for hh in range(hp):
                c_ref[hh] = carries[hh]
                a_ref[hh] = accs[hh]
            for hh, cols in enumerate(heads):
                q = qs[hh]

                def cond(s):
                    return jnp.logical_and(s[0] < n_prev, s[1] > SB_STOP_LOG)

                def body(s, hh=hh, cols=cols, q=q):
                    ks = pl.ds(pl.multiple_of(r0 - (s[0] + 1) * blk, 16), blk)
                    c, a = _sb_step(q, k_ref[ks, cols], v_ref[ks, cols], upper, None, c_ref[hh], a_ref[hh])
                    c_ref[hh] = c
                    a_ref[hh] = a
                    return s[0] + 1, jnp.max(c)

                _, cmax = lax.while_loop(cond, body, (jnp.int32(n_stat), jnp.max(carries[hh])))
                if head_cols:
                    @pl.when(cmax > SB_STOP_LOG)
                    def _(hh=hh, cols=cols, q=q):
                        _, a = _sb_step(q, k_ref[0:blk, cols], v_ref[0:blk, cols], upper, col < head_cols,
                                        c_ref[hh], a_ref[hh])
                        a_ref[hh] = a
            accs = [a_ref[hh] for hh in range(hp)]
        for hh, cols in enumerate(heads):
            zg = z_ref[rows, cols].astype(F32)
            o_ref[rows, cols] = (accs[hh] * (zg * _sigmoid(zg))).astype(BF16)

    n_lead = min(n_static, n_full)
    for i in range(n_lead):
        q_block(i * blk, i, i, 0)

    def body(i, carry):
        q_block(pl.multiple_of(i * blk, blk), i, n_static, 0)
        return carry

    lax.fori_loop(n_lead, n_full, body, 0)
    if rem:
        q_block(T - blk, n_full - 1, min(n_static, n_full - 1), rem)


def _sb_prompt(p3, *, blk=SB_BLOCK, hp=SB_HEADS_PER_STEP, n_static=SB_STATIC_BLOCKS):
    B, T, _ = p3.shape
    w = hp * SB_HEAD_DIM
    spec = lambda c: pl.BlockSpec((None, T, w), lambda b, g: (b, 0, c // w + g))
    return pl.pallas_call(
        functools.partial(_sb_prompt_kernel, blk=blk, hp=hp, n_static=n_static),
        grid=(B, SB_HEADS // hp),
        in_specs=[spec(COL_SQ), spec(COL_SK), spec(COL_SV), spec(COL_SZ)],
        out_specs=pl.BlockSpec((None, T, w), lambda b, g: (b, 0, g)),
        out_shape=jax.ShapeDtypeStruct((B, T, SB_W), BF16),
        scratch_shapes=[pltpu.VMEM((hp, blk, 1), F32), pltpu.VMEM((hp, blk, SB_HEAD_DIM), F32)],
        compiler_params=pltpu.CompilerParams(
            dimension_semantics=("arbitrary", "arbitrary"), vmem_limit_bytes=VMEM_LIMIT),
    )(p3, p3, p3, p3)


def _sb_sample_kernel(q_ref, k_ref, v_ref, z_ref, pk_ref, pv_ref, pk_hbm, pv_hbm, o_ref,
                      kbuf, vbuf, c_ref, a_ref, sem, *, blk, layer):
    b = pl.program_id(0)
    tq = q_ref.shape[0]
    d = SB_HEAD_DIM
    tail_len = pk_ref.shape[0]
    n_stat = tail_len // blk
    past_len = pk_hbm.shape[2]
    n_past = past_len // blk
    row = lax.broadcasted_iota(jnp.int32, (tq, tq), 0)
    col = lax.broadcasted_iota(jnp.int32, (tq, tq), 1)
    upper_q = _strict_upper(tq)
    upper = _strict_upper(blk)
    scale = SB_HEAD_DIM ** -0.5

    def fetch(s0):
        ck = pltpu.make_async_copy(pk_hbm.at[layer, b, pl.ds(s0, blk)], kbuf, sem.at[0])
        cv = pltpu.make_async_copy(pv_hbm.at[layer, b, pl.ds(s0, blk)], vbuf, sem.at[1])
        ck.start()
        cv.start()
        ck.wait()
        cv.wait()

    heads = [slice(h * d, (h + 1) * d) for h in range(SB_HEADS)]
    qs, carries, accs = [], [], []
    for h, cols in enumerate(heads):
        q = (q_ref[:, cols].astype(F32) * scale).astype(BF16)
        carry = jnp.zeros((tq, 1), F32)
        acc = jnp.zeros((tq, d), F32)
        carry, acc = _sb_step(q, k_ref[:, cols], v_ref[:, cols], upper_q, col < row, carry, acc)
        for n in range(n_stat):
            ks = pl.ds(tail_len - (n + 1) * blk, blk)
            carry, acc = _sb_step(q, pk_ref[ks, h, :].astype(BF16), pv_ref[ks, h, :].astype(BF16),
                                  upper, None, carry, acc)
        qs.append(q)
        carries.append(carry)
        accs.append(acc)
    if n_past > n_stat:
        for h in range(SB_HEADS):
            c_ref[h] = carries[h]
            a_ref[h] = accs[h]
        for h in range(SB_HEADS):
            def cond(s):
                return jnp.logical_and(s[0] < n_past, s[1] > SB_STOP_LOG)

            def body(s, h=h, q=qs[h]):
                fetch(pl.multiple_of(past_len - (s[0] + 1) * blk, blk))
                c, a = _sb_step(q, kbuf[:, h, :].astype(BF16), vbuf[:, h, :].astype(BF16), upper, None,
                                c_ref[h], a_ref[h])
                c_ref[h] = c
                a_ref[h] = a
                return s[0] + 1, jnp.max(c)

            lax.while_loop(cond, body, (jnp.int32(n_stat), jnp.max(carries[h])))
        accs = [a_ref[h] for h in range(SB_HEADS)]
    for h, cols in enumerate(heads):
        zg = z_ref[:, cols].astype(F32)
        o_ref[:, cols] = (accs[h] * (zg * _sigmoid(zg))).astype(BF16)


def _sb_sample(p3, past_k, past_v, *, layer, blk=SB_BLOCK, n_static=SB_STATIC_BLOCKS):
    B, T, _ = p3.shape
    past_len = past_k.shape[2]
    assert past_len % blk == 0 and T % 16 == 0
    n_static = min(n_static, past_len // blk)
    tail = n_static * blk
    assert past_len % tail == 0
    spec = lambda c: pl.BlockSpec((None, T, SB_W), lambda b: (b, 0, c // SB_W))
    past_tail = pl.BlockSpec((None, None, tail, SB_HEADS, SB_HEAD_DIM),
                             lambda b: (layer, b, past_len // tail - 1, 0, 0))
    hbm = pl.BlockSpec(memory_space=pl.ANY)
    return pl.pallas_call(
        functools.partial(_sb_sample_kernel, blk=blk, layer=layer),
        grid=(B,),
        in_specs=[spec(COL_SQ), spec(COL_SK), spec(COL_SV), spec(COL_SZ), past_tail, past_tail, hbm, hbm],
        out_specs=pl.BlockSpec((None, T, SB_W), lambda b: (b, 0, 0)),
        out_shape=jax.ShapeDtypeStruct((B, T, SB_W), BF16),
        scratch_shapes=[
            pltpu.VMEM((blk, SB_HEADS, SB_HEAD_DIM), F32),
            pltpu.VMEM((blk, SB_HEADS, SB_HEAD_DIM), F32),
            pltpu.VMEM((SB_HEADS, T, 1), F32),
            pltpu.VMEM((SB_HEADS, T, SB_HEAD_DIM), F32),
            pltpu.SemaphoreType.DMA((2,)),
        ],
        compiler_params=pltpu.CompilerParams(
            dimension_semantics=("arbitrary",), vmem_limit_bytes=VMEM_LIMIT),
    )(p3, p3, p3, p3, past_k, past_v, past_k, past_v)


def _out_proj_kernel(og_ref, os_ref, mg_ref, ms_ref, x_ref, wg_ref, ws_ref, wo_ref, g_ref, y_ref):
    y_gla = _dot(og_ref[...], wg_ref[...])
    y_sb = _dot(os_ref[...], ws_ref[...])
    merged = (_sigmoid(mg_ref[...].astype(F32)) * y_gla + _sigmoid(ms_ref[...].astype(F32)) * y_sb)
    u = _dot(merged.astype(BF16), wo_ref[...])
    un = u * lax.rsqrt(jnp.mean(u * u, axis=-1, keepdims=True) + EPS) * g_ref[...]
    y_ref[...] = x_ref[...] + un


def _out_proj(og, osb, p, x, w_up_gla, w_up_sb, w_o, post_gain, *, row_target):
    n = x.shape[0]
    tm = _row_tile(n, row_target)
    const = lambda i: (0, 0)
    cm = D_MODEL
    return pl.pallas_call(
        _out_proj_kernel,
        grid=(n // tm,),
        in_specs=[
            pl.BlockSpec((tm, GLA_V), lambda i: (i, 0)),
            pl.BlockSpec((tm, SB_W), lambda i: (i, 0)),
            pl.BlockSpec((tm, cm), lambda i: (i, COL_MG // cm)),
            pl.BlockSpec((tm, cm), lambda i: (i, COL_MS // cm)),
            pl.BlockSpec((tm, cm), lambda i: (i, 0)),
            pl.BlockSpec((GLA_V, cm), const),
            pl.BlockSpec((SB_W, cm), const),
            pl.BlockSpec((cm, cm), const),
            pl.BlockSpec((1, cm), const),
        ],
        out_specs=pl.BlockSpec((tm, cm), lambda i: (i, 0)),
        out_shape=jax.ShapeDtypeStruct((n, cm), F32),
        compiler_params=pltpu.CompilerParams(
            dimension_semantics=("arbitrary",), vmem_limit_bytes=VMEM_LIMIT),
    )(og, osb, p, p, x, w_up_gla, w_up_sb, w_o, post_gain)


def _pack_weights(pre_gain, w_in, w_a2, b_a, gla_gain, w_up_gla, w_up_sb, w_o, post_gain):
    sizes = (GLA_QK, GLA_QK, GLA_V, GLA_V, GLA_RANK, SB_W, SB_W, SB_W, SB_W, D_MODEL, D_MODEL)
    offs = [0]
    for s in sizes:
        offs.append(offs[-1] + s)
    part = lambda i: w_in[:, offs[i]:offs[i + 1]]
    g_q, g_k, g_v, g_z, g_r, s_q, s_k, s_v, s_z, m_gla, m_sb = (part(i) for i in range(11))
    w_main = jnp.concatenate([s_k, s_v, s_q, s_z, g_v, g_z, m_gla, m_sb, g_q, g_k], axis=1).astype(BF16)
    w_r = jnp.pad(g_r, ((0, 0), (0, LANES - GLA_RANK))).astype(BF16)
    w_a2p = jnp.pad(w_a2.astype(F32), ((0, LANES - GLA_RANK), (0, 0)))
    return dict(
        pre_gain=pre_gain.reshape(1, D_MODEL).astype(F32), w_main=w_main, w_r=w_r, w_a2p=w_a2p,
        b_a=b_a.reshape(1, GLA_QK).astype(F32), gla_gain=gla_gain.astype(F32),
        w_up_gla=w_up_gla.astype(BF16), w_up_sb=w_up_sb.astype(BF16), w_o=w_o.astype(BF16),
        post_gain=post_gain.reshape(1, D_MODEL).astype(F32))


def _layer(x, batch, past, s0, w, kv_prev, *, layer, depth, in_rows, out_rows):
    n = x.shape[0]
    T = n // batch
    p, k_all, v_all, la, la_min = _in_proj(
        x, w["pre_gain"], w["w_main"], w["w_r"], w["w_a2p"], w["b_a"], kv_prev,
        layer=layer, depth=depth, row_target=in_rows)
    safe = (jnp.min(la_min) >= GLA_SAFE_LOG_DECAY).astype(jnp.int32).reshape(1)
    p3 = p.reshape(batch, T, P_COLS)
    og, s_new = _gla(p3, la.reshape(batch, T, GLA_QK), s0, w["gla_gain"], safe)
    if past is None:
        osb = _sb_prompt(p3)
    else:
        osb = _sb_sample(p3, past[0], past[1], layer=layer)
    y = _out_proj(og.reshape(n, GLA_V), osb.reshape(n, SB_W), p, x,
                  w["w_up_gla"], w["w_up_sb"], w["w_o"], w["post_gain"], row_target=out_rows)
    return y, (k_all, v_all), s_new


def kernel(x_prompt, x_sample, cache_sb_k, cache_sb_v, state_gla, meta_tokens, pre_gain, w_in, w_a2, b_a,
           gla_gain, w_up_gla, w_up_sb, w_o, post_gain):
    B, seq, _ = x_prompt.shape
    Bs, seq_s, _ = x_sample.shape
    depth = w_in.shape[0]
    T = N_META + seq
    meta = jnp.broadcast_to(meta_tokens.astype(x_prompt.dtype)[None], (B, N_META, D_MODEL))
    hp = jnp.concatenate([meta, x_prompt], axis=1).reshape(B * T, D_MODEL)
    hs = x_sample.reshape(Bs * seq_s, D_MODEL)
    zero_state = jnp.zeros((B, GLA_HEADS, GLA_DK, GLA_DV), F32)
    kv_p, kv_s, sp, ss_ = None, None, [], []
    for l in range(depth):
        w = _pack_weights(pre_gain[l], w_in[l], w_a2[l], b_a[l], gla_gain[l], w_up_gla[l], w_up_sb[l],
                          w_o[l], post_gain[l])
        hp, kv_p, s_ = _layer(hp, B, None, zero_state, w, kv_p,
                              layer=l, depth=depth, in_rows=768, out_rows=256)
        sp.append(s_)
        hs, kv_s, s_ = _layer(hs, Bs, (cache_sb_k, cache_sb_v), state_gla[l], w, kv_s,
                              layer=l, depth=depth, in_rows=512, out_rows=256)
        ss_.append(s_)
    y_prompt = hp.reshape(B, T, D_MODEL)[:, N_META:]
    y_sample = hs.reshape(Bs, seq_s, D_MODEL)
    shape_p = (depth, B, T, SB_HEADS, SB_HEAD_DIM)
    shape_s = (depth, Bs, seq_s, SB_HEADS, SB_HEAD_DIM)
    return (y_prompt, y_sample, kv_p[0].reshape(shape_p), kv_p[1].reshape(shape_p), jnp.stack(sp),
            kv_s[0].reshape(shape_s), kv_s[1].reshape(shape_s), jnp.stack(ss_))
```

```python
import functools

import jax
import jax.numpy as jnp
from jax import lax
from jax.experimental import pallas as pl
from jax.experimental.pallas import tpu as pltpu

F32 = jnp.float32
BF16 = jnp.bfloat16

D_MODEL = 2048
N_META = 16
EPS = 1e-6
GLA_HEADS = 4
GLA_DK = 128
GLA_DV = 256
GLA_RANK = 16
GLA_TAU = 16.0
SB_HEADS = 8
SB_HEAD_DIM = 128
GLA_QK = GLA_HEADS * GLA_DK
GLA_V = GLA_HEADS * GLA_DV
SB_W = SB_HEADS * SB_HEAD_DIM

LANES = 128
PROJ_TILE = 1024
P_COLS = 11 * PROJ_TILE
COL_GQ, COL_GK, COL_GV, COL_GZ, COL_SQ, COL_SK, COL_SV, COL_SZ, COL_MG, COL_MS = (
    0, 512, 1024, 2048, 3072, 4096, 5120, 6144, 7168, 9216)
OUT_ROW_PARTS = 2

GLA_CHUNK = 64
GLA_HEADS_PER_STEP = 4
GLA_SAFE_LOG_DECAY = -1.0
SB_BLOCK = 128
SB_HEADS_PER_STEP = 4
SB_STATIC_BLOCKS = 2
SB_STOP_LOG = -120.0
VMEM_LIMIT = 56 * 1024 * 1024


def _row_tile(n, target):
    best = None
    for d in range(16, min(n, target) + 1, 16):
        if n % d == 0:
            best = d
    return best if best is not None else n


def _sigmoid(x):
    return 1.0 / (1.0 + jnp.exp(-x))


def _log_sigmoid(x):
    return jnp.minimum(x, 0.0) - jnp.log(1.0 + jnp.exp(-jnp.abs(x)))


def _dot(a, b):
    return jnp.dot(a, b, preferred_element_type=F32)


def _dot_nt(a, b):
    return lax.dot_general(a, b, (((1,), (1,)), ((), ())), preferred_element_type=F32)


def _dot_tn(a, b):
    return lax.dot_general(a, b, (((0,), (0,)), ((), ())), preferred_element_type=F32)


def _split3(x):
    hi = x.astype(BF16)
    r = x - hi.astype(F32)
    mid = r.astype(BF16)
    lo = (r - mid.astype(F32)).astype(BF16)
    return hi, mid, lo


def _in_proj_kernel(x_ref, g_ref, w_ref, wr_ref, wa2_ref, ba_ref, *rest, sub, aliased):
    if aliased:
        rest = rest[2:]
    p_ref, k_ref, v_ref, la_ref, lamin_ref, h_ref = rest
    j = pl.program_id(1)
    tm = x_ref.shape[0]

    @pl.when(j == 0)
    def _():
        gain = g_ref[...]

        def body(r, carry):
            rows = pl.ds(pl.multiple_of(r * sub, sub), sub)
            x = x_ref[rows, :]
            ms = jnp.mean(x * x, axis=-1, keepdims=True)
            h_ref[rows, :] = (x * lax.rsqrt(ms + EPS) * gain).astype(BF16)
            return carry

        lax.fori_loop(0, tm // sub, body, 0, unroll=2 if (tm // sub) % 2 == 0 else 1)
        g_r = _dot(h_ref[...], wr_ref[...])
        hi, mid, _ = _split3(g_r)
        w2h, w2m, _ = _split3(wa2_ref[...])
        xa = _dot(hi, w2h) + _dot(hi, w2m) + _dot(mid, w2h) + ba_ref[...]
        la = _log_sigmoid(xa) * (1.0 / GLA_TAU)
        la_ref[...] = la
        m = jnp.min(jnp.min(la, axis=0, keepdims=True), axis=1, keepdims=True)
        lamin_ref[...] = jnp.broadcast_to(m.reshape(1, 1, 1), lamin_ref.shape)

    acc = _dot(h_ref[...], w_ref[...])
    p_ref[...] = acc.astype(BF16)

    def heads_out(ref):
        for h in range(SB_HEADS):
            ref[pl.ds(h, tm, stride=SB_HEADS), :] = acc[:, h * SB_HEAD_DIM:(h + 1) * SB_HEAD_DIM]

    @pl.when(j == COL_SK // PROJ_TILE)
    def _():
        heads_out(k_ref)

    @pl.when(j == COL_SV // PROJ_TILE)
    def _():
        heads_out(v_ref)


def _in_proj(x, gain, w_main, w_r, w_a2p, b_a, kv_prev, *, layer, depth, row_target):
    n = x.shape[0]
    tm = _row_tile(n, row_target)
    nt = n // tm
    sub = _row_tile(tm, 64)
    grid = (nt, P_COLS // PROJ_TILE)
    const = lambda i, j: (0, 0)
    aliased = kv_prev is not None
    kv_spec = pl.BlockSpec((None, tm * SB_HEADS, SB_HEAD_DIM), lambda i, j: (layer, i, 0))
    kv_shape = jax.ShapeDtypeStruct((depth, n * SB_HEADS, SB_HEAD_DIM), F32)
    in_specs = [
        pl.BlockSpec((tm, D_MODEL), lambda i, j: (i, 0)),
        pl.BlockSpec((1, D_MODEL), const),
        pl.BlockSpec((D_MODEL, PROJ_TILE), lambda i, j: (0, j)),
        pl.BlockSpec((D_MODEL, LANES), const),
        pl.BlockSpec((LANES, GLA_QK), const),
        pl.BlockSpec((1, GLA_QK), const),
    ]
    args = [x, gain, w_main, w_r, w_a2p, b_a]
    aliases = {}
    if aliased:
        in_specs += [pl.BlockSpec(memory_space=pl.ANY), pl.BlockSpec(memory_space=pl.ANY)]
        args += list(kv_prev)
        aliases = {6: 1, 7: 2}
    return pl.pallas_call(
        functools.partial(_in_proj_kernel, sub=sub, aliased=aliased),
        grid=grid,
        in_specs=in_specs,
        out_specs=[
            pl.BlockSpec((tm, PROJ_TILE), lambda i, j: (i, j)),
            kv_spec,
            kv_spec,
            pl.BlockSpec((tm, GLA_QK), lambda i, j: (i, 0)),
            pl.BlockSpec((1, 8, LANES), lambda i, j: (i, 0, 0)),
        ],
        out_shape=[
            jax.ShapeDtypeStruct((n, P_COLS), BF16),
            kv_shape,
            kv_shape,
            jax.ShapeDtypeStruct((n, GLA_QK), F32),
            jax.ShapeDtypeStruct((nt, 8, LANES), F32),
        ],
        scratch_shapes=[pltpu.VMEM((tm, D_MODEL), BF16)],
        input_output_aliases=aliases,
        compiler_params=pltpu.CompilerParams(
            dimension_semantics=("arbitrary", "arbitrary"), vmem_limit_bytes=VMEM_LIMIT),
    )(*args)


def _gla_kernel(safe_ref, q_ref, k_ref, v_ref, z_ref, la_ref, s0_ref, gain_ref,
                o_ref, sout_ref, st_ref, oi_ref, qs_ref, bs_ref, *, chunk, hp):
    C = chunk
    T = q_ref.shape[0]
    n_full, rem = T // C, T % C
    safe = safe_ref[0] > 0

    row = lax.broadcasted_iota(jnp.int32, (C, C), 0)
    col = lax.broadcasted_iota(jnp.int32, (C, C), 1)
    causal = row >= col
    tri = jnp.where(causal, 1.0, 0.0).astype(BF16)
    tri3 = jnp.concatenate([tri, tri, tri], axis=1)
    row1 = lax.broadcasted_iota(jnp.int32, (C, 1), 0)

    for hh in range(hp):
        st_ref[hh] = s0_ref[hh].T

    def group(chunks, fast):
        items = [(hh, off, pad) for off, pad in chunks for hh in range(hp)]
        kcols = lambda hh: slice(hh * GLA_DK, (hh + 1) * GLA_DK)
        vcols = lambda hh: slice(hh * GLA_DV, (hh + 1) * GLA_DV)
        qs, ks, vs, las = [], [], [], []
        for hh, off, pad in items:
            rows = pl.ds(off, C)
            q = q_ref[rows, kcols(hh)].astype(F32) * (GLA_DK ** -0.5)
            k = k_ref[rows, kcols(hh)].astype(F32)
            v = v_ref[rows, vcols(hh)]
            la = la_ref[rows, kcols(hh)]
            if pad:
                valid = row1 >= pad
                la = jnp.where(valid, la, 0.0)
                k = jnp.where(valid, k, 0.0)
                v = jnp.where(valid, v, jnp.zeros_like(v))
            qs.append(q)
            ks.append(k)
            vs.append(v)
            las.append(jnp.concatenate(_split3(la), axis=0))
        bs = [_dot(tri3, la3) for la3 in las]
        qes, kds, ends = [], [], []
        for q, k, b in zip(qs, ks, bs):
            b_end = b[C - 1:C, :]
            qes.append((q * jnp.exp(b)).astype(BF16))
            kds.append((k * jnp.exp(b_end - b)).astype(BF16))
            ends.append(jnp.exp(b_end))
        if fast:
            kes = [(k * jnp.exp(-b)).astype(BF16) for k, b in zip(ks, bs)]
            ss = [jnp.where(causal, _dot_nt(qe, ke), 0.0).astype(BF16) for qe, ke in zip(qes, kes)]
            intra = [_dot(s, v) for s, v in zip(ss, vs)]
        else:
            intra = []
            for q, k, v, b in zip(qs, ks, vs, bs):
                qs_ref[...] = q
                bs_ref[...] = b
                vf = v.astype(F32)

                def body(t, carry, k=k, b=b, vf=vf):
                    bt = bs_ref[pl.ds(t, 1), :]
                    qt = qs_ref[pl.ds(t, 1), :]
                    diff = jnp.where(row1 <= t, bt - b, -jnp.inf)
                    w = jnp.sum(k * jnp.exp(diff) * qt, axis=-1, keepdims=True)
                    oi_ref[pl.ds(t, 1), :] = jnp.sum(w * vf, axis=0, keepdims=True)
                    return carry

                lax.fori_loop(0, C, body, 0)
                intra.append(oi_ref[...])
        kvs = [_dot_tn(v, kd) for v, kd in zip(vs, kds)]
        sts = [st_ref[hh] for hh in range(hp)]
        os_ = []
        for i, (hh, off, pad) in enumerate(items):
            os_.append(_dot_nt(qes[i], sts[hh].astype(BF16)) + intra[i])
            sts[hh] = sts[hh] * ends[i] + kvs[i]
        for hh in range(hp):
            st_ref[hh] = sts[hh]
        for (hh, off, pad), o in zip(items, os_):
            on = o * lax.rsqrt(jnp.mean(o * o, axis=-1, keepdims=True) + EPS) * gain_ref[hh]
            z = z_ref[pl.ds(off, C), vcols(hh)].astype(F32)
            out = (on * (z * _sigmoid(z))).astype(BF16)
            if pad:
                o_ref[pl.ds(off + pad, C - pad), vcols(hh)] = out[pad:, :]
            else:
                o_ref[pl.ds(off, C), vcols(hh)] = out

    def run(fast):
        per = 2 if n_full % 2 == 0 else 1

        def body(c, carry):
            base = pl.multiple_of(c * (per * C), per * C)
            group([(base + i * C, 0) for i in range(per)], fast)
            return carry

        lax.fori_loop(0, n_full // per, body, 0)
        if rem:
            group([(T - C, C - rem)], fast)

    pl.when(safe)(lambda: run(True))
    pl.when(jnp.logical_not(safe))(lambda: run(False))
    for hh in range(hp):
        sout_ref[hh] = st_ref[hh].T


def _gla(p3, la3, s0, gla_gain, safe, *, chunk=GLA_CHUNK, hp=GLA_HEADS_PER_STEP):
    B, T, _ = p3.shape
    H = GLA_HEADS
    wk, wv = hp * GLA_DK, hp * GLA_DV
    kq, kk = COL_GQ // wk, COL_GK // wk
    kv, kz = COL_GV // wv, COL_GZ // wv
    grid_spec = pltpu.PrefetchScalarGridSpec(
        num_scalar_prefetch=1,
        grid=(B, H // hp),
        in_specs=[
            pl.BlockSpec((None, T, wk), lambda b, g, s: (b, 0, kq + g)),
            pl.BlockSpec((None, T, wk), lambda b, g, s: (b, 0, kk + g)),
            pl.BlockSpec((None, T, wv), lambda b, g, s: (b, 0, kv + g)),
            pl.BlockSpec((None, T, wv), lambda b, g, s: (b, 0, kz + g)),
            pl.BlockSpec((None, T, wk), lambda b, g, s: (b, 0, g)),
            pl.BlockSpec((None, hp, GLA_DK, GLA_DV), lambda b, g, s: (b, g, 0, 0)),
            pl.BlockSpec((hp, 1, GLA_DV), lambda b, g, s: (g, 0, 0)),
        ],
        out_specs=[
            pl.BlockSpec((None, T, wv), lambda b, g, s: (b, 0, g)),
            pl.BlockSpec((None, hp, GLA_DK, GLA_DV), lambda b, g, s: (b, g, 0, 0)),
        ],
        scratch_shapes=[
            pltpu.VMEM((hp, GLA_DV, GLA_DK), F32),
            pltpu.VMEM((chunk, GLA_DV), F32),
            pltpu.VMEM((chunk, GLA_DK), F32),
            pltpu.VMEM((chunk, GLA_DK), F32),
        ],
    )
    return pl.pallas_call(
        functools.partial(_gla_kernel, chunk=chunk, hp=hp),
        grid_spec=grid_spec,
        out_shape=[
            jax.ShapeDtypeStruct((B, T, GLA_V), BF16),
            jax.ShapeDtypeStruct((B, H, GLA_DK, GLA_DV), F32),
        ],
        compiler_params=pltpu.CompilerParams(
            dimension_semantics=("arbitrary", "arbitrary"), vmem_limit_bytes=VMEM_LIMIT),
    )(safe, p3, p3, p3, p3, la3, s0, gla_gain.reshape(H, 1, GLA_DV))


def _strict_upper(n):
    j = lax.broadcasted_iota(jnp.int32, (n, n), 0)
    s = lax.broadcasted_iota(jnp.int32, (n, n), 1)
    return jnp.where(j > s, 1.0, 0.0).astype(BF16)


def _sb_step(q, kb, vb, upper, mask, carry, acc):
    z = _dot_nt(q, kb)
    lb = _log_sigmoid(z)
    lom = lb - z
    if mask is not None:
        lom = jnp.where(mask, lom, 0.0)
    hi = lom.astype(BF16)
    lo = (lom - hi.astype(F32)).astype(BF16)
    tail = _dot(hi, upper) + _dot(lo, upper) + carry
    w = jnp.exp(lb + tail)
    if mask is not None:
        w = jnp.where(mask, w, 0.0)
    acc = acc + _dot(w.astype(BF16), vb)
    carry = carry + jnp.sum(lom, axis=-1, keepdims=True)
    return carry, acc


def _sb_prompt_kernel(q_ref, k_ref, v_ref, z_ref, o_ref, c_ref, a_ref, *, blk, hp, n_static):
    T = q_ref.shape[0]
    d = SB_HEAD_DIM
    n_full, rem = T // blk, T % blk
    upper = _strict_upper(blk)
    upper2 = jnp.concatenate([upper, upper], axis=0)
    row = lax.broadcasted_iota(jnp.int32, (blk, blk), 0)
    col = lax.broadcasted_iota(jnp.int32, (blk, blk), 1)
    strict = col < row
    scale = SB_HEAD_DIM ** -0.5

    def q_block(r0, n_prev, n_stat, head_cols):
        rows = pl.ds(r0, blk)
        heads = [slice(hh * d, (hh + 1) * d) for hh in range(hp)]
        nb = n_stat + 1
        k0 = r0 - n_stat * blk
        krows = pl.ds(k0 if isinstance(k0, int) else pl.multiple_of(k0, blk), nb * blk)
        qs = [(q_ref[rows, cols].astype(F32) * scale).astype(BF16) for cols in heads]
        zs = [_dot_nt(q, k_ref[krows, cols]) for q, cols in zip(qs, heads)]
        lbs, hls, sums = [], [], []
        for z in zs:
            lb = _log_sigmoid(z)
            lom = lb - z
            hl, rs = [], []
            for a in range(nb):
                lom_a = lom[:, a * blk:(a + 1) * blk]
                if a == nb - 1:
                    lom_a = jnp.where(strict, lom_a, 0.0)
                hi = lom_a.astype(BF16)
                lo = (lom_a - hi.astype(F32)).astype(BF16)
                hl.append(jnp.concatenate([hi, lo], axis=1))
                rs.append(jnp.sum(lom_a, axis=-1, keepdims=True))
            lbs.append(lb)
            hls.append(hl)
            sums.append(rs)
        tails = [[_dot(hl_a, upper2) for hl_a in hl] for hl in hls]
        ws, carries = [], []
        for lb, tl, rs in zip(lbs, tails, sums):
            carry = jnp.zeros((blk, 1), F32)
            w = [None] * nb
            for a in range(nb - 1, -1, -1):
                w_a = jnp.exp(lb[:, a * blk:(a + 1) * blk] + tl[a] + carry)
                if a == nb - 1:
                    w_a = jnp.where(strict, w_a, 0.0)
                w[a] = w_a.astype(BF16)
                carry = carry + rs[a]
            ws.append(jnp.concatenate(w, axis=1) if nb > 1 else w[0])
            carries.append(carry)
        accs = [_dot(w, v_ref[krows, cols]) for w, cols in zip(ws, heads)]
        if head_cols or not (isinstance(n_prev, int) and n_prev == n_stat):
            for hh in range(hp):
                c_ref[hh] = carries[hh]
                a_ref[hh] = accs[hh]
            for hh, cols in enumerate(heads):
                q = qs[hh]

                def cond(s):
                    return jnp.logical_and(s[0] < n_prev, s[1] > SB_STOP_LOG)

                def body(s, hh=hh, cols=cols, q=q):
                    ks = pl.ds(pl.multiple_of(r0 - (s[0] + 1) * blk, 16), blk)
                    c, a = _sb_step(q, k_ref[ks, cols], v_ref[ks, cols], upper, None, c_ref[hh], a_ref[hh])
                    c_ref[hh] = c
                    a_ref[hh] = a
                    return s[0] + 1, jnp.max(c)

                _, cmax = lax.while_loop(cond, body, (jnp.int32(n_stat), jnp.max(carries[hh])))
                if head_cols:
                    @pl.when(cmax > SB_STOP_LOG)
                    def _(hh=hh, cols=cols, q=q):
                        _, a = _sb_step(q, k_ref[0:blk, cols], v_ref[0:blk, cols], upper, col < head_cols,
                                        c_ref[hh], a_ref[hh])
                        a_ref[hh] = a
            accs = [a_ref[hh] for hh in range(hp)]
        for hh, cols in enumerate(heads):
            zg = z_ref[rows, cols].astype(F32)
            o_ref[rows, cols] = (accs[hh] * (zg * _sigmoid(zg))).astype(BF16)

    n_lead = min(n_static, n_full)
    for i in range(n_lead):
        q_block(i * blk, i, i, 0)

    def body(i, carry):
        q_block(pl.multiple_of(i * blk, blk), i, n_static, 0)
        return carry

    lax.fori_loop(n_lead, n_full, body, 0)
    if rem:
        q_block(T - blk, n_full - 1, min(n_static, n_full - 1), rem)


def _sb_prompt(p3, *, blk=SB_BLOCK, hp=SB_HEADS_PER_STEP, n_static=SB_STATIC_BLOCKS):
    B, T, _ = p3.shape
    w = hp * SB_HEAD_DIM
    spec = lambda c: pl.BlockSpec((None, T, w), lambda b, g: (b, 0, c // w + g))
    return pl.pallas_call(
        functools.partial(_sb_prompt_kernel, blk=blk, hp=hp, n_static=n_static),
        grid=(B, SB_HEADS // hp),
        in_specs=[spec(COL_SQ), spec(COL_SK), spec(COL_SV), spec(COL_SZ)],
        out_specs=pl.BlockSpec((None, T, w), lambda b, g: (b, 0, g)),
        out_shape=jax.ShapeDtypeStruct((B, T, SB_W), BF16),
        scratch_shapes=[pltpu.VMEM((hp, blk, 1), F32), pltpu.VMEM((hp, blk, SB_HEAD_DIM), F32)],
        compiler_params=pltpu.CompilerParams(
            dimension_semantics=("arbitrary", "arbitrary"), vmem_limit_bytes=VMEM_LIMIT),
    )(p3, p3, p3, p3)


def _sb_sample_kernel(q_ref, k_ref, v_ref, z_ref, pk_ref, pv_ref, pk_hbm, pv_hbm, o_ref,
                      kbuf, vbuf, c_ref, a_ref, sem, *, blk, layer):
    b = pl.program_id(0)
    tq = q_ref.shape[0]
    d = SB_HEAD_DIM
    nh = SB_HEADS
    tail_len = pk_ref.shape[0] // nh
    n_stat = tail_len // blk
    past_len = pk_hbm.shape[2] // nh
    n_past = past_len // blk
    row = lax.broadcasted_iota(jnp.int32, (tq, tq), 0)
    col = lax.broadcasted_iota(jnp.int32, (tq, tq), 1)
    upper_q = _strict_upper(tq)
    upper = _strict_upper(blk)
    scale = SB_HEAD_DIM ** -0.5

    def fetch(s0):
        src = pl.ds(pl.multiple_of(s0 * nh, blk * nh), blk * nh)
        ck = pltpu.make_async_copy(pk_hbm.at[layer, b, src], kbuf, sem.at[0])
        cv = pltpu.make_async_copy(pv_hbm.at[layer, b, src], vbuf, sem.at[1])
        ck.start()
        cv.start()
        ck.wait()
        cv.wait()

    heads = [slice(h * d, (h + 1) * d) for h in range(SB_HEADS)]
    qs, carries, accs = [], [], []
    for h, cols in enumerate(heads):
        q = (q_ref[:, cols].astype(F32) * scale).astype(BF16)
        carry = jnp.zeros((tq, 1), F32)
        acc = jnp.zeros((tq, d), F32)
        carry, acc = _sb_step(q, k_ref[:, cols], v_ref[:, cols], upper_q, col < row, carry, acc)
        for n in range(n_stat):
            ks = pl.ds((tail_len - (n + 1) * blk) * nh + h, blk, stride=nh)
            carry, acc = _sb_step(q, pk_ref[ks, :].astype(BF16), pv_ref[ks, :].astype(BF16),
                                  upper, None, carry, acc)
        qs.append(q)
        carries.append(carry)
        accs.append(acc)
    if n_past > n_stat:
        for h in range(SB_HEADS):
            c_ref[h] = carries[h]
            a_ref[h] = accs[h]
        for h in range(SB_HEADS):
            def cond(s):
                return jnp.logical_and(s[0] < n_past, s[1] > SB_STOP_LOG)

            def body(s, h=h, q=qs[h]):
                fetch(pl.multiple_of(past_len - (s[0] + 1) * blk, blk))
                hs = pl.ds(h, blk, stride=nh)
                c, a = _sb_step(q, kbuf[hs, :].astype(BF16), vbuf[hs, :].astype(BF16), upper, None,
                                c_ref[h], a_ref[h])
                c_ref[h] = c
                a_ref[h] = a
                return s[0] + 1, jnp.max(c)

            lax.while_loop(cond, body, (jnp.int32(n_stat), jnp.max(carries[h])))
        accs = [a_ref[h] for h in range(SB_HEADS)]
    for h, cols in enumerate(heads):
        zg = z_ref[:, cols].astype(F32)
        o_ref[:, cols] = (accs[h] * (zg * _sigmoid(zg))).astype(BF16)


def _sb_sample(p3, past_k, past_v, *, layer, blk=SB_BLOCK, n_static=SB_STATIC_BLOCKS):
    B, T, _ = p3.shape
    nh = SB_HEADS
    past_len = past_k.shape[2] // nh
    assert past_len % blk == 0 and T % 16 == 0
    n_static = min(n_static, past_len // blk)
    tail = n_static * blk
    assert past_len % tail == 0
    spec = lambda c: pl.BlockSpec((None, T, SB_W), lambda b: (b, 0, c // SB_W))
    past_tail = pl.BlockSpec((None, None, tail * nh, SB_HEAD_DIM),
                             lambda b: (layer, b, past_len // tail - 1, 0))
    hbm = pl.BlockSpec(memory_space=pl.ANY)
    return pl.pallas_call(
        functools.partial(_sb_sample_kernel, blk=blk, layer=layer),
        grid=(B,),
        in_specs=[spec(COL_SQ), spec(COL_SK), spec(COL_SV), spec(COL_SZ), past_tail, past_tail, hbm, hbm],
        out_specs=pl.BlockSpec((None, T, SB_W), lambda b: (b, 0, 0)),
        out_shape=jax.ShapeDtypeStruct((B, T, SB_W), BF16),
        scratch_shapes=[
            pltpu.VMEM((blk * nh, SB_HEAD_DIM), F32),
            pltpu.VMEM((blk * nh, SB_HEAD_DIM), F32),
            pltpu.VMEM((SB_HEADS, T, 1), F32),
            pltpu.VMEM((SB_HEADS, T, SB_HEAD_DIM), F32),
            pltpu.SemaphoreType.DMA((2,)),
        ],
        compiler_params=pltpu.CompilerParams(
            dimension_semantics=("arbitrary",), vmem_limit_bytes=VMEM_LIMIT),
    )(p3, p3, p3, p3, past_k, past_v, past_k, past_v)


def _out_proj_kernel(og_ref, os_ref, mg0_ref, mg1_ref, ms0_ref, ms1_ref, x_ref, wg_ref, ws_ref, wo_ref,
                     g_ref, y_ref, *, parts):
    rp = x_ref.shape[0] // parts
    groups = [pl.ds(i * rp, rp) for i in range(parts)]
    hw = D_MODEL // 2
    gate = lambda ref, r: _sigmoid(ref[r, :].astype(F32))
    y_gla = [_dot(og_ref[r, :], wg_ref[...]) for r in groups]
    y_sb = [_dot(os_ref[r, :], ws_ref[...]) for r in groups]
    merged = []
    for r, yg, ys in zip(groups, y_gla, y_sb):
        lo = gate(mg0_ref, r) * yg[:, :hw] + gate(ms0_ref, r) * ys[:, :hw]
        hi = gate(mg1_ref, r) * yg[:, hw:] + gate(ms1_ref, r) * ys[:, hw:]
        merged.append(jnp.concatenate([lo, hi], axis=1).astype(BF16))
    us = [_dot(m, wo_ref[...]) for m in merged]
    for r, u in zip(groups, us):
        un = u * lax.rsqrt(jnp.mean(u * u, axis=-1, keepdims=True) + EPS) * g_ref[...]
        y_ref[r, :] = x_ref[r, :] + un


def _out_proj_tail(og, osb, p, x, w_up_gla, w_up_sb, w_o, post_gain, *, batch, lead, row_target):
    T = x.shape[0] // batch
    rows = T - lead
    tm = _row_tile(rows, row_target)
    nt = rows // tm
    const = lambda b, i: (0, 0)
    cm = D_MODEL
    assert T % 16 == 0 and lead % 16 == 0 and tm % 16 == 0
    at = lambda col: (lambda b, i: (pl.multiple_of(b * T + lead + i * tm, 16), col))
    tile = lambda cols: (pl.Element(tm), pl.Element(cols))
    hw = cm // 2
    return pl.pallas_call(
        functools.partial(_out_proj_kernel, parts=OUT_ROW_PARTS),
        grid=(batch, nt),
        in_specs=[
            pl.BlockSpec(tile(GLA_V), at(0)),
            pl.BlockSpec(tile(SB_W), at(0)),
            pl.BlockSpec(tile(hw), at(COL_MG)),
            pl.BlockSpec(tile(hw), at(COL_MG + hw)),
            pl.BlockSpec(tile(hw), at(COL_MS)),
            pl.BlockSpec(tile(hw), at(COL_MS + hw)),
            pl.BlockSpec(tile(cm), at(0)),
            pl.BlockSpec((GLA_V, cm), const),
            pl.BlockSpec((SB_W, cm), const),
            pl.BlockSpec((cm, cm), const),
            pl.BlockSpec((1, cm), const),
        ],
        out_specs=pl.BlockSpec((tm, cm), lambda b, i: (b * nt + i, 0)),
        out_shape=jax.ShapeDtypeStruct((batch * rows, cm), F32),
        compiler_params=pltpu.CompilerParams(
            dimension_semantics=("arbitrary", "arbitrary"), vmem_limit_bytes=VMEM_LIMIT),
    )(og, osb, p, p, p, p, x, w_up_gla, w_up_sb, w_o, post_gain).reshape(batch, rows, cm)


def _out_proj(og, osb, p, x, w_up_gla, w_up_sb, w_o, post_gain, *, row_target):
    n = x.shape[0]
    tm = _row_tile(n, row_target)
    const = lambda i: (0, 0)
    cm = D_MODEL
    hw = cm // 2
    half = lambda c: pl.BlockSpec((tm, hw), lambda i: (i, c // hw))
    return pl.pallas_call(
        functools.partial(_out_proj_kernel, parts=OUT_ROW_PARTS),
        grid=(n // tm,),
        in_specs=[
            pl.BlockSpec((tm, GLA_V), lambda i: (i, 0)),
            pl.BlockSpec((tm, SB_W), lambda i: (i, 0)),
            half(COL_MG), half(COL_MG + hw), half(COL_MS), half(COL_MS + hw),
            pl.BlockSpec((tm, cm), lambda i: (i, 0)),
            pl.BlockSpec((GLA_V, cm), const),
            pl.BlockSpec((SB_W, cm), const),
            pl.BlockSpec((cm, cm), const),
            pl.BlockSpec((1, cm), const),
        ],
        out_specs=pl.BlockSpec((tm, cm), lambda i: (i, 0)),
        out_shape=jax.ShapeDtypeStruct((n, cm), F32),
        compiler_params=pltpu.CompilerParams(
            dimension_semantics=("arbitrary",), vmem_limit_bytes=VMEM_LIMIT),
    )(og, osb, p, p, p, p, x, w_up_gla, w_up_sb, w_o, post_gain)


def _pack_weights(pre_gain, w_in, w_a2, b_a, gla_gain, w_up_gla, w_up_sb, w_o, post_gain):
    sizes = (GLA_QK, GLA_QK, GLA_V, GLA_V, GLA_RANK, SB_W, SB_W, SB_W, SB_W, D_MODEL, D_MODEL)
    offs = [0]
    for s in sizes:
        offs.append(offs[-1] + s)
    r0, r1 = offs[4], offs[5]
    w_main = jnp.concatenate([w_in[:, :r0], w_in[:, r1:]], axis=1).astype(BF16)
    w_r = jnp.pad(w_in[:, r0:r1], ((0, 0), (0, LANES - GLA_RANK))).astype(BF16)
    w_a2p = jnp.pad(w_a2.astype(F32), ((0, LANES - GLA_RANK), (0, 0)))
    return dict(
        pre_gain=pre_gain.reshape(1, D_MODEL).astype(F32), w_main=w_main, w_r=w_r, w_a2p=w_a2p,
        b_a=b_a.reshape(1, GLA_QK).astype(F32), gla_gain=gla_gain.astype(F32),
        w_up_gla=w_up_gla.astype(BF16), w_up_sb=w_up_sb.astype(BF16), w_o=w_o.astype(BF16),
        post_gain=post_gain.reshape(1, D_MODEL).astype(F32))


def _layer(x, batch, past, s0, w, kv_prev, *, layer, depth, in_rows, out_rows, drop_lead=0):
    n = x.shape[0]
    T = n // batch
    p, k_all, v_all, la, la_min = _in_proj(
        x, w["pre_gain"], w["w_main"], w["w_r"], w["w_a2p"], w["b_a"], kv_prev,
        layer=layer, depth=depth, row_target=in_rows)
    safe = (jnp.min(la_min) >= GLA_SAFE_LOG_DECAY).astype(jnp.int32).reshape(1)
    p3 = p.reshape(batch, T, P_COLS)
    og, s_new = _gla(p3, la.reshape(batch, T, GLA_QK), s0, w["gla_gain"], safe)
    if past is None:
        osb = _sb_prompt(p3)
    else:
        osb = _sb_sample(p3, past[0], past[1], layer=layer)
    tail_w = (w["w_up_gla"], w["w_up_sb"], w["w_o"], w["post_gain"])
    og, osb = og.reshape(n, GLA_V), osb.reshape(n, SB_W)
    if drop_lead:
        y = _out_proj_tail(og, osb, p, x, *tail_w, batch=batch, lead=drop_lead, row_target=out_rows)
    else:
        y = _out_proj(og, osb, p, x, *tail_w, row_target=out_rows)
    return y, (k_all, v_all), s_new


def kernel(x_prompt, x_sample, cache_sb_k, cache_sb_v, state_gla, meta_tokens, pre_gain, w_in, w_a2, b_a,
           gla_gain, w_up_gla, w_up_sb, w_o, post_gain):
    B, seq, _ = x_prompt.shape
    Bs, seq_s, _ = x_sample.shape
    depth = w_in.shape[0]
    T = N_META + seq
    meta = jnp.broadcast_to(meta_tokens.astype(x_prompt.dtype)[None], (B, N_META, D_MODEL))
    hp = jnp.concatenate([meta, x_prompt], axis=1).reshape(B * T, D_MODEL)
    hs = x_sample.reshape(Bs * seq_s, D_MODEL)
    zero_state = jnp.zeros((B, GLA_HEADS, GLA_DK, GLA_DV), F32)
    cache_rows = (depth, Bs, cache_sb_k.shape[2] * SB_HEADS, SB_HEAD_DIM)
    cache_k2, cache_v2 = cache_sb_k.reshape(cache_rows), cache_sb_v.reshape(cache_rows)
    kv_p, kv_s, sp, ss_ = None, None, [], []
    for l in range(depth):
        w = _pack_weights(pre_gain[l], w_in[l], w_a2[l], b_a[l], gla_gain[l], w_up_gla[l], w_up_sb[l],
                          w_o[l], post_gain[l])
        hp, kv_p, s_ = _layer(hp, B, None, zero_state, w, kv_p, layer=l, depth=depth, in_rows=768,
                              out_rows=256, drop_lead=N_META if l == depth - 1 else 0)
        sp.append(s_)
        hs, kv_s, s_ = _layer(hs, Bs, (cache_k2, cache_v2), state_gla[l], w, kv_s,
                              layer=l, depth=depth, in_rows=512, out_rows=256)
        ss_.append(s_)
    y_prompt = hp
    y_sample = hs.reshape(Bs, seq_s, D_MODEL)
    shape_p = (depth, B, T, SB_HEADS, SB_HEAD_DIM)
    shape_s = (depth, Bs, seq_s, SB_HEADS, SB_HEAD_DIM)
    return (y_prompt, y_sample, kv_p[0].reshape(shape_p), kv_p[1].reshape(shape_p), jnp.stack(sp),
            kv_s[0].reshape(shape_s), kv_s[1].reshape(shape_s), jnp.stack(ss_))
```

```python
import functools

import jax
import jax.numpy as jnp
from jax import lax
from jax.experimental import pallas as pl
from jax.experimental.pallas import tpu as pltpu

F32 = jnp.float32
BF16 = jnp.bfloat16

D_MODEL = 2048
N_META = 16
EPS = 1e-6
GLA_HEADS = 4
GLA_DK = 128
GLA_DV = 256
GLA_RANK = 16
GLA_TAU = 16.0
SB_HEADS = 8
SB_HEAD_DIM = 128
GLA_QK = GLA_HEADS * GLA_DK
GLA_V = GLA_HEADS * GLA_DV
SB_W = SB_HEADS * SB_HEAD_DIM

LANES = 128
PROJ_TILE = 1024
P_COLS = 11 * PROJ_TILE
COL_GQ, COL_GK, COL_GV, COL_GZ, COL_SQ, COL_SK, COL_SV, COL_SZ, COL_MG, COL_MS = (
    0, 512, 1024, 2048, 3072, 4096, 5120, 6144, 7168, 9216)
OUT_ROW_PARTS = 2

GLA_CHUNK = 64
GLA_HEADS_PER_STEP = 4
GLA_SAFE_LOG_DECAY = -1.0
SB_BLOCK = 128
SB_HEADS_PER_STEP = 4
SB_STATIC_BLOCKS = 2
SB_STOP_LOG = -120.0
VMEM_LIMIT = 56 * 1024 * 1024


def _row_tile(n, target):
    best = None
    for d in range(16, min(n, target) + 1, 16):
        if n % d == 0:
            best = d
    return best if best is not None else n


def _sigmoid(x):
    return 1.0 / (1.0 + jnp.exp(-x))


def _log_sigmoid(x):
    return jnp.minimum(x, 0.0) - jnp.log(1.0 + jnp.exp(-jnp.abs(x)))


def _dot(a, b):
    return jnp.dot(a, b, preferred_element_type=F32)


def _dot_nt(a, b):
    return lax.dot_general(a, b, (((1,), (1,)), ((), ())), preferred_element_type=F32)


def _dot_tn(a, b):
    return lax.dot_general(a, b, (((0,), (0,)), ((), ())), preferred_element_type=F32)


def _split3(x):
    hi = x.astype(BF16)
    r = x - hi.astype(F32)
    mid = r.astype(BF16)
    lo = (r - mid.astype(F32)).astype(BF16)
    return hi, mid, lo


def _in_proj_kernel(x_ref, g_ref, w_ref, wr_ref, wa2_ref, ba_ref, *rest, sub, aliased):
    if aliased:
        rest = rest[2:]
    p_ref, k_ref, v_ref, la_ref, lamin_ref, h_ref = rest
    j = pl.program_id(1)
    tm = x_ref.shape[0]

    @pl.when(j == 0)
    def _():
        gain = g_ref[...]

        def body(r, carry):
            rows = pl.ds(pl.multiple_of(r * sub, sub), sub)
            x = x_ref[rows, :]
            ms = jnp.mean(x * x, axis=-1, keepdims=True)
            h_ref[rows, :] = (x * lax.rsqrt(ms + EPS) * gain).astype(BF16)
            return carry

        lax.fori_loop(0, tm // sub, body, 0, unroll=2 if (tm // sub) % 2 == 0 else 1)
        g_r = _dot(h_ref[...], wr_ref[...])
        hi, mid, _ = _split3(g_r)
        w2h, w2m, _ = _split3(wa2_ref[...])
        xa = _dot(hi, w2h) + _dot(hi, w2m) + _dot(mid, w2h) + ba_ref[...]
        la = _log_sigmoid(xa) * (1.0 / GLA_TAU)
        la_ref[...] = la
        m = jnp.min(jnp.min(la, axis=0, keepdims=True), axis=1, keepdims=True)
        lamin_ref[...] = jnp.broadcast_to(m.reshape(1, 1, 1), lamin_ref.shape)

    acc = _dot(h_ref[...], w_ref[...])
    p_ref[...] = acc.astype(BF16)

    def heads_out(ref):
        for h in range(SB_HEADS):
            ref[pl.ds(h, tm, stride=SB_HEADS), :] = acc[:, h * SB_HEAD_DIM:(h + 1) * SB_HEAD_DIM]

    @pl.when(j == COL_SK // PROJ_TILE)
    def _():
        heads_out(k_ref)

    @pl.when(j == COL_SV // PROJ_TILE)
    def _():
        heads_out(v_ref)


def _in_proj(x, gain, w_main, w_r, w_a2p, b_a, kv_prev, *, layer, depth, row_target):
    n = x.shape[0]
    tm = _row_tile(n, row_target)
    nt = n // tm
    sub = _row_tile(tm, 64)
    grid = (nt, P_COLS // PROJ_TILE)
    const = lambda i, j: (0, 0)
    aliased = kv_prev is not None
    kv_spec = pl.BlockSpec((None, tm * SB_HEADS, SB_HEAD_DIM), lambda i, j: (layer, i, 0))
    kv_shape = jax.ShapeDtypeStruct((depth, n * SB_HEADS, SB_HEAD_DIM), F32)
    in_specs = [
        pl.BlockSpec((tm, D_MODEL), lambda i, j: (i, 0)),
        pl.BlockSpec((1, D_MODEL), const),
        pl.BlockSpec((D_MODEL, PROJ_TILE), lambda i, j: (0, j)),
        pl.BlockSpec((D_MODEL, LANES), const),
        pl.BlockSpec((LANES, GLA_QK), const),
        pl.BlockSpec((1, GLA_QK), const),
    ]
    args = [x, gain, w_main, w_r, w_a2p, b_a]
    aliases = {}
    if aliased:
        in_specs += [pl.BlockSpec(memory_space=pl.ANY), pl.BlockSpec(memory_space=pl.ANY)]
        args += list(kv_prev)
        aliases = {6: 1, 7: 2}
    return pl.pallas_call(
        functools.partial(_in_proj_kernel, sub=sub, aliased=aliased),
        grid=grid,
        in_specs=in_specs,
        out_specs=[
            pl.BlockSpec((tm, PROJ_TILE), lambda i, j: (i, j)),
            kv_spec,
            kv_spec,
            pl.BlockSpec((tm, GLA_QK), lambda i, j: (i, 0)),
            pl.BlockSpec((1, 8, LANES), lambda i, j: (i, 0, 0)),
        ],
        out_shape=[
            jax.ShapeDtypeStruct((n, P_COLS), BF16),
            kv_shape,
            kv_shape,
            jax.ShapeDtypeStruct((n, GLA_QK), F32),
            jax.ShapeDtypeStruct((nt, 8, LANES), F32),
        ],
        scratch_shapes=[pltpu.VMEM((tm, D_MODEL), BF16)],
        input_output_aliases=aliases,
        compiler_params=pltpu.CompilerParams(
            dimension_semantics=("arbitrary", "arbitrary"), vmem_limit_bytes=VMEM_LIMIT),
    )(*args)


def _gla_kernel(safe_ref, q_ref, k_ref, v_ref, z_ref, la_ref, s0_ref, gain_ref,
                o_ref, sout_ref, st_ref, oi_ref, qs_ref, bs_ref, *, chunk, hp):
    C = chunk
    T = q_ref.shape[0]
    n_full, rem = T // C, T % C
    safe = safe_ref[0] > 0

    row = lax.broadcasted_iota(jnp.int32, (C, C), 0)
    col = lax.broadcasted_iota(jnp.int32, (C, C), 1)
    causal = row >= col
    tri = jnp.where(causal, 1.0, 0.0).astype(BF16)
    tri3 = jnp.concatenate([tri, tri, tri], axis=1)
    row1 = lax.broadcasted_iota(jnp.int32, (C, 1), 0)

    for hh in range(hp):
        st_ref[hh] = s0_ref[hh].T

    def group(chunks, fast):
        items = [(hh, off, pad) for off, pad in chunks for hh in range(hp)]
        kcols = lambda hh: slice(hh * GLA_DK, (hh + 1) * GLA_DK)
        vcols = lambda hh: slice(hh * GLA_DV, (hh + 1) * GLA_DV)
        qs, ks, vs, las = [], [], [], []
        for hh, off, pad in items:
            rows = pl.ds(off, C)
            q = q_ref[rows, kcols(hh)].astype(F32) * (GLA_DK ** -0.5)
            k = k_ref[rows, kcols(hh)].astype(F32)
            v = v_ref[rows, vcols(hh)]
            la = la_ref[rows, kcols(hh)]
            if pad:
                valid = row1 >= pad
                la = jnp.where(valid, la, 0.0)
                k = jnp.where(valid, k, 0.0)
                v = jnp.where(valid, v, jnp.zeros_like(v))
            qs.append(q)
            ks.append(k)
            vs.append(v)
            las.append(jnp.concatenate(_split3(la), axis=0))
        bs = [_dot(tri3, la3) for la3 in las]
        qes, kds, ends = [], [], []
        for q, k, b in zip(qs, ks, bs):
            b_end = b[C - 1:C, :]
            qes.append((q * jnp.exp(b)).astype(BF16))
            kds.append((k * jnp.exp(b_end - b)).astype(BF16))
            ends.append(jnp.exp(b_end))
        if fast:
            kes = [(k * jnp.exp(-b)).astype(BF16) for k, b in zip(ks, bs)]
            ss = [jnp.where(causal, _dot_nt(qe, ke), 0.0).astype(BF16) for qe, ke in zip(qes, kes)]
            intra = [_dot(s, v) for s, v in zip(ss, vs)]
        else:
            intra = []
            for q, k, v, b in zip(qs, ks, vs, bs):
                qs_ref[...] = q
                bs_ref[...] = b
                vf = v.astype(F32)

                def body(t, carry, k=k, b=b, vf=vf):
                    bt = bs_ref[pl.ds(t, 1), :]
                    qt = qs_ref[pl.ds(t, 1), :]
                    diff = jnp.where(row1 <= t, bt - b, -jnp.inf)
                    w = jnp.sum(k * jnp.exp(diff) * qt, axis=-1, keepdims=True)
                    oi_ref[pl.ds(t, 1), :] = jnp.sum(w * vf, axis=0, keepdims=True)
                    return carry

                lax.fori_loop(0, C, body, 0)
                intra.append(oi_ref[...])
        kvs = [_dot_tn(v, kd) for v, kd in zip(vs, kds)]
        sts = [st_ref[hh] for hh in range(hp)]
        os_ = []
        for i, (hh, off, pad) in enumerate(items):
            os_.append(_dot_nt(qes[i], sts[hh].astype(BF16)) + intra[i])
            sts[hh] = sts[hh] * ends[i] + kvs[i]
        for hh in range(hp):
            st_ref[hh] = sts[hh]
        for (hh, off, pad), o in zip(items, os_):
            on = o * lax.rsqrt(jnp.mean(o * o, axis=-1, keepdims=True) + EPS) * gain_ref[hh]
            z = z_ref[pl.ds(off, C), vcols(hh)].astype(F32)
            out = (on * (z * _sigmoid(z))).astype(BF16)
            if pad:
                o_ref[pl.ds(off + pad, C - pad), vcols(hh)] = out[pad:, :]
            else:
                o_ref[pl.ds(off, C), vcols(hh)] = out

    def run(fast):
        per = max(c for c in (1, 2, 4) if n_full % c == 0)

        def body(c, carry):
            base = pl.multiple_of(c * (per * C), per * C)
            group([(base + i * C, 0) for i in range(per)], fast)
            return carry

        lax.fori_loop(0, n_full // per, body, 0)
        if rem:
            group([(T - C, C - rem)], fast)

    pl.when(safe)(lambda: run(True))
    pl.when(jnp.logical_not(safe))(lambda: run(False))
    for hh in range(hp):
        sout_ref[hh] = st_ref[hh].T


def _gla(p3, la3, s0, gla_gain, safe, *, chunk=GLA_CHUNK, hp=GLA_HEADS_PER_STEP):
    B, T, _ = p3.shape
    H = GLA_HEADS
    wk, wv = hp * GLA_DK, hp * GLA_DV
    kq, kk = COL_GQ // wk, COL_GK // wk
    kv, kz = COL_GV // wv, COL_GZ // wv
    grid_spec = pltpu.PrefetchScalarGridSpec(
        num_scalar_prefetch=1,
        grid=(B, H // hp),
        in_specs=[
            pl.BlockSpec((None, T, wk), lambda b, g, s: (b, 0, kq + g)),
            pl.BlockSpec((None, T, wk), lambda b, g, s: (b, 0, kk + g)),
            pl.BlockSpec((None, T, wv), lambda b, g, s: (b, 0, kv + g)),
            pl.BlockSpec((None, T, wv), lambda b, g, s: (b, 0, kz + g)),
            pl.BlockSpec((None, T, wk), lambda b, g, s: (b, 0, g)),
            pl.BlockSpec((None, hp, GLA_DK, GLA_DV), lambda b, g, s: (b, g, 0, 0)),
            pl.BlockSpec((hp, 1, GLA_DV), lambda b, g, s: (g, 0, 0)),
        ],
        out_specs=[
            pl.BlockSpec((None, T, wv), lambda b, g, s: (b, 0, g)),
            pl.BlockSpec((None, hp, GLA_DK, GLA_DV), lambda b, g, s: (b, g, 0, 0)),
        ],
        scratch_shapes=[
            pltpu.VMEM((hp, GLA_DV, GLA_DK), F32),
            pltpu.VMEM((chunk, GLA_DV), F32),
            pltpu.VMEM((chunk, GLA_DK), F32),
            pltpu.VMEM((chunk, GLA_DK), F32),
        ],
    )
    return pl.pallas_call(
        functools.partial(_gla_kernel, chunk=chunk, hp=hp),
        grid_spec=grid_spec,
        out_shape=[
            jax.ShapeDtypeStruct((B, T, GLA_V), BF16),
            jax.ShapeDtypeStruct((B, H, GLA_DK, GLA_DV), F32),
        ],
        compiler_params=pltpu.CompilerParams(
            dimension_semantics=("arbitrary", "arbitrary"), vmem_limit_bytes=VMEM_LIMIT),
    )(safe, p3, p3, p3, p3, la3, s0, gla_gain.reshape(H, 1, GLA_DV))


def _strict_upper(n):
    j = lax.broadcasted_iota(jnp.int32, (n, n), 0)
    s = lax.broadcasted_iota(jnp.int32, (n, n), 1)
    return jnp.where(j > s, 1.0, 0.0).astype(BF16)


def _sb_step(q, kb, vb, upper, mask, carry, acc):
    z = _dot_nt(q, kb)
    lb = _log_sigmoid(z)
    lom = lb - z
    if mask is not None:
        lom = jnp.where(mask, lom, 0.0)
    hi = lom.astype(BF16)
    lo = (lom - hi.astype(F32)).astype(BF16)
    tail = _dot(hi, upper) + _dot(lo, upper) + carry
    w = jnp.exp(lb + tail)
    if mask is not None:
        w = jnp.where(mask, w, 0.0)
    acc = acc + _dot(w.astype(BF16), vb)
    carry = carry + jnp.sum(lom, axis=-1, keepdims=True)
    return carry, acc


def _sb_prompt_kernel(q_ref, k_ref, v_ref, z_ref, o_ref, c_ref, a_ref, *, blk, hp, n_static):
    T = q_ref.shape[0]
    d = SB_HEAD_DIM
    n_full, rem = T // blk, T % blk
    upper = _strict_upper(blk)
    upper2 = jnp.concatenate([upper, upper], axis=0)
    row = lax.broadcasted_iota(jnp.int32, (blk, blk), 0)
    col = lax.broadcasted_iota(jnp.int32, (blk, blk), 1)
    strict = col < row
    scale = SB_HEAD_DIM ** -0.5

    def q_block(r0, n_prev, n_stat, head_cols):
        rows = pl.ds(r0, blk)
        heads = [slice(hh * d, (hh + 1) * d) for hh in range(hp)]
        nb = n_stat + 1
        k0 = r0 - n_stat * blk
        krows = pl.ds(k0 if isinstance(k0, int) else pl.multiple_of(k0, blk), nb * blk)
        qs = [(q_ref[rows, cols].astype(F32) * scale).astype(BF16) for cols in heads]
        zs = [_dot_nt(q, k_ref[krows, cols]) for q, cols in zip(qs, heads)]
        lbs, hls, sums = [], [], []
        for z in zs:
            lb = _log_sigmoid(z)
            lom = lb - z
            hl, rs = [], []
            for a in range(nb):
                lom_a = lom[:, a * blk:(a + 1) * blk]
                if a == nb - 1:
                    lom_a = jnp.where(strict, lom_a, 0.0)
                hi = lom_a.astype(BF16)
                lo = (lom_a - hi.astype(F32)).astype(BF16)
                hl.append(jnp.concatenate([hi, lo], axis=1))
                rs.append(jnp.sum(lom_a, axis=-1, keepdims=True))
            lbs.append(lb)
            hls.append(hl)
            sums.append(rs)
        tails = [[_dot(hl_a, upper2) for hl_a in hl] for hl in hls]
        ws, carries = [], []
        for lb, tl, rs in zip(lbs, tails, sums):
            carry = jnp.zeros((blk, 1), F32)
            w = [None] * nb
            for a in range(nb - 1, -1, -1):
                w_a = jnp.exp(lb[:, a * blk:(a + 1) * blk] + tl[a] + carry)
                if a == nb - 1:
                    w_a = jnp.where(strict, w_a, 0.0)
                w[a] = w_a.astype(BF16)
                carry = carry + rs[a]
            ws.append(jnp.concatenate(w, axis=1) if nb > 1 else w[0])
            carries.append(carry)
        accs = [_dot(w, v_ref[krows, cols]) for w, cols in zip(ws, heads)]

        def emit(hh, acc):
            zg = z_ref[rows, heads[hh]].astype(F32)
            o_ref[rows, heads[hh]] = (acc * (zg * _sigmoid(zg))).astype(BF16)

        for hh in range(hp):
            emit(hh, accs[hh])
        if not (head_cols or not (isinstance(n_prev, int) and n_prev == n_stat)):
            return
        worst = functools.reduce(jnp.maximum, carries)

        @pl.when(jnp.max(worst) > SB_STOP_LOG)
        def _():
            for hh in range(hp):
                c_ref[hh] = carries[hh]
                a_ref[hh] = accs[hh]
            for hh, cols in enumerate(heads):
                q = qs[hh]

                def cond(s):
                    return jnp.logical_and(s[0] < n_prev, s[1] > SB_STOP_LOG)

                def body(s, hh=hh, cols=cols, q=q):
                    ks = pl.ds(pl.multiple_of(r0 - (s[0] + 1) * blk, 16), blk)
                    c, a = _sb_step(q, k_ref[ks, cols], v_ref[ks, cols], upper, None, c_ref[hh], a_ref[hh])
                    c_ref[hh] = c
                    a_ref[hh] = a
                    return s[0] + 1, jnp.max(c)

                _, cmax = lax.while_loop(cond, body, (jnp.int32(n_stat), jnp.max(carries[hh])))
                if head_cols:
                    @pl.when(cmax > SB_STOP_LOG)
                    def _(hh=hh, cols=cols, q=q):
                        _, a = _sb_step(q, k_ref[0:blk, cols], v_ref[0:blk, cols], upper, col < head_cols,
                                        c_ref[hh], a_ref[hh])
                        a_ref[hh] = a
                emit(hh, a_ref[hh])

    n_lead = min(n_static, n_full)
    for i in range(n_lead):
        q_block(i * blk, i, i, 0)

    def body(i, carry):
        q_block(pl.multiple_of(i * blk, blk), i, n_static, 0)
        return carry

    lax.fori_loop(n_lead, n_full, body, 0)
    if rem:
        q_block(T - blk, n_full - 1, min(n_static, n_full - 1), rem)


def _sb_prompt(p3, *, blk=SB_BLOCK, hp=SB_HEADS_PER_STEP, n_static=SB_STATIC_BLOCKS):
    B, T, _ = p3.shape
    w = hp * SB_HEAD_DIM
    spec = lambda c: pl.BlockSpec((None, T, w), lambda b, g: (b, 0, c // w + g))
    return pl.pallas_call(
        functools.partial(_sb_prompt_kernel, blk=blk, hp=hp, n_static=n_static),
        grid=(B, SB_HEADS // hp),
        in_specs=[spec(COL_SQ), spec(COL_SK), spec(COL_SV), spec(COL_SZ)],
        out_specs=pl.BlockSpec((None, T, w), lambda b, g: (b, 0, g)),
        out_shape=jax.ShapeDtypeStruct((B, T, SB_W), BF16),
        scratch_shapes=[pltpu.VMEM((hp, blk, 1), F32), pltpu.VMEM((hp, blk, SB_HEAD_DIM), F32)],
        compiler_params=pltpu.CompilerParams(
            dimension_semantics=("arbitrary", "arbitrary"), vmem_limit_bytes=VMEM_LIMIT),
    )(p3, p3, p3, p3)


def _sb_sample_kernel(q_ref, k_ref, v_ref, z_ref, pk_ref, pv_ref, pk_hbm, pv_hbm, o_ref,
                      kbuf, vbuf, c_ref, a_ref, sem, *, blk, layer):
    b = pl.program_id(0)
    tq = q_ref.shape[0]
    d = SB_HEAD_DIM
    nh = SB_HEADS
    tail_len = pk_ref.shape[0] // nh
    n_stat = tail_len // blk
    past_len = pk_hbm.shape[2] // nh
    n_past = past_len // blk
    row = lax.broadcasted_iota(jnp.int32, (tq, tq), 0)
    col = lax.broadcasted_iota(jnp.int32, (tq, tq), 1)
    upper_q = _strict_upper(tq)
    upper = _strict_upper(blk)
    scale = SB_HEAD_DIM ** -0.5

    def fetch(s0):
        src = pl.ds(pl.multiple_of(s0 * nh, blk * nh), blk * nh)
        ck = pltpu.make_async_copy(pk_hbm.at[layer, b, src], kbuf, sem.at[0])
        cv = pltpu.make_async_copy(pv_hbm.at[layer, b, src], vbuf, sem.at[1])
        ck.start()
        cv.start()
        ck.wait()
        cv.wait()

    heads = [slice(h * d, (h + 1) * d) for h in range(SB_HEADS)]
    strict = col < row
    nb = n_stat + 1
    past_rows = lambda h, a: pl.ds((tail_len - (n_stat - a) * blk) * nh + h, blk, stride=nh)
    kbs = [[pk_ref[past_rows(h, a), :].astype(BF16) for a in range(n_stat)] + [k_ref[:, cols]]
           for h, cols in enumerate(heads)]
    vbs = [[pv_ref[past_rows(h, a), :].astype(BF16) for a in range(n_stat)] + [v_ref[:, cols]]
           for h, cols in enumerate(heads)]
    uppers = [jnp.concatenate([upper, upper], axis=0)] * n_stat + [jnp.concatenate([upper_q, upper_q], axis=0)]
    qs = [(q_ref[:, cols].astype(F32) * scale).astype(BF16) for cols in heads]
    zs = [[_dot_nt(q, kb) for kb in kb_h] for q, kb_h in zip(qs, kbs)]
    lbs, hls, sums = [], [], []
    for z_h in zs:
        lb_h, hl_h, rs_h = [], [], []
        for a, z in enumerate(z_h):
            lb = _log_sigmoid(z)
            lom = lb - z
            if a == nb - 1:
                lom = jnp.where(strict, lom, 0.0)
            hi = lom.astype(BF16)
            lo = (lom - hi.astype(F32)).astype(BF16)
            lb_h.append(lb)
            hl_h.append(jnp.concatenate([hi, lo], axis=1))
            rs_h.append(jnp.sum(lom, axis=-1, keepdims=True))
        lbs.append(lb_h)
        hls.append(hl_h)
        sums.append(rs_h)
    tails = [[_dot(hl, up) for hl, up in zip(hl_h, uppers)] for hl_h in hls]
    ws, carries = [], []
    for lb_h, tl_h, rs_h in zip(lbs, tails, sums):
        carry = jnp.zeros((tq, 1), F32)
        w_h = [None] * nb
        for a in range(nb - 1, -1, -1):
            w = jnp.exp(lb_h[a] + tl_h[a] + carry)
            if a == nb - 1:
                w = jnp.where(strict, w, 0.0)
            w_h[a] = w.astype(BF16)
            carry = carry + rs_h[a]
        ws.append(w_h)
        carries.append(carry)
    accs = [functools.reduce(jnp.add, [_dot(w, vb) for w, vb in zip(w_h, vb_h)]) for w_h, vb_h in zip(ws, vbs)]

    def emit(h, acc):
        zg = z_ref[:, heads[h]].astype(F32)
        o_ref[:, heads[h]] = (acc * (zg * _sigmoid(zg))).astype(BF16)

    for h in range(SB_HEADS):
        emit(h, accs[h])
    if n_past == n_stat:
        return
    worst = functools.reduce(jnp.maximum, carries)

    @pl.when(jnp.max(worst) > SB_STOP_LOG)
    def _():
        for h in range(SB_HEADS):
            c_ref[h] = carries[h]
            a_ref[h] = accs[h]
        for h in range(SB_HEADS):
            def cond(s):
                return jnp.logical_and(s[0] < n_past, s[1] > SB_STOP_LOG)

            def body(s, h=h, q=qs[h]):
                fetch(pl.multiple_of(past_len - (s[0] + 1) * blk, blk))
                hs = pl.ds(h, blk, stride=nh)
                c, a = _sb_step(q, kbuf[hs, :].astype(BF16), vbuf[hs, :].astype(BF16), upper, None,
                                c_ref[h], a_ref[h])
                c_ref[h] = c
                a_ref[h] = a
                return s[0] + 1, jnp.max(c)

            lax.while_loop(cond, body, (jnp.int32(n_stat), jnp.max(carries[h])))
            emit(h, a_ref[h])


def _sb_sample(p3, past_k, past_v, *, layer, blk=SB_BLOCK, n_static=SB_STATIC_BLOCKS):
    B, T, _ = p3.shape
    nh = SB_HEADS
    past_len = past_k.shape[2] // nh
    assert past_len % blk == 0 and T % 16 == 0
    n_static = min(n_static, past_len // blk)
    tail = n_static * blk
    assert past_len % tail == 0
    spec = lambda c: pl.BlockSpec((None, T, SB_W), lambda b: (b, 0, c // SB_W))
    past_tail = pl.BlockSpec((None, None, tail * nh, SB_HEAD_DIM),
                             lambda b: (layer, b, past_len // tail - 1, 0))
    hbm = pl.BlockSpec(memory_space=pl.ANY)
    return pl.pallas_call(
        functools.partial(_sb_sample_kernel, blk=blk, layer=layer),
        grid=(B,),
        in_specs=[spec(COL_SQ), spec(COL_SK), spec(COL_SV), spec(COL_SZ), past_tail, past_tail, hbm, hbm],
        out_specs=pl.BlockSpec((None, T, SB_W), lambda b: (b, 0, 0)),
        out_shape=jax.ShapeDtypeStruct((B, T, SB_W), BF16),
        scratch_shapes=[
            pltpu.VMEM((blk * nh, SB_HEAD_DIM), F32),
            pltpu.VMEM((blk * nh, SB_HEAD_DIM), F32),
            pltpu.VMEM((SB_HEADS, T, 1), F32),
            pltpu.VMEM((SB_HEADS, T, SB_HEAD_DIM), F32),
            pltpu.SemaphoreType.DMA((2,)),
        ],
        compiler_params=pltpu.CompilerParams(
            dimension_semantics=("arbitrary",), vmem_limit_bytes=VMEM_LIMIT),
    )(p3, p3, p3, p3, past_k, past_v, past_k, past_v)


def _out_proj_kernel(og_ref, os_ref, mg0_ref, mg1_ref, ms0_ref, ms1_ref, x_ref, wg_ref, ws_ref, wo_ref,
                     g_ref, y_ref, *, parts):
    rp = x_ref.shape[0] // parts
    groups = [pl.ds(i * rp, rp) for i in range(parts)]
    hw = D_MODEL // 2
    gate = lambda ref, r: _sigmoid(ref[r, :].astype(F32))
    y_gla = [_dot(og_ref[r, :], wg_ref[...]) for r in groups]
    y_sb = [_dot(os_ref[r, :], ws_ref[...]) for r in groups]
    merged = []
    for r, yg, ys in zip(groups, y_gla, y_sb):
        lo = gate(mg0_ref, r) * yg[:, :hw] + gate(ms0_ref, r) * ys[:, :hw]
        hi = gate(mg1_ref, r) * yg[:, hw:] + gate(ms1_ref, r) * ys[:, hw:]
        merged.append(jnp.concatenate([lo, hi], axis=1).astype(BF16))
    us = [_dot(m, wo_ref[...]) for m in merged]
    for r, u in zip(groups, us):
        un = u * lax.rsqrt(jnp.mean(u * u, axis=-1, keepdims=True) + EPS) * g_ref[...]
        y_ref[r, :] = x_ref[r, :] + un


def _out_proj_tail(og, osb, p, x, w_up_gla, w_up_sb, w_o, post_gain, *, batch, lead, row_target):
    T = x.shape[0] // batch
    rows = T - lead
    tm = _row_tile(rows, row_target)
    nt = rows // tm
    const = lambda b, i: (0, 0)
    cm = D_MODEL
    assert T % 16 == 0 and lead % 16 == 0 and tm % 16 == 0
    at = lambda col: (lambda b, i: (pl.multiple_of(b * T + lead + i * tm, 16), col))
    resident = lambda shape: pl.BlockSpec(shape, const, pipeline_mode=pl.Buffered(1))
    tile = lambda cols: (pl.Element(tm), pl.Element(cols))
    hw = cm // 2
    return pl.pallas_call(
        functools.partial(_out_proj_kernel, parts=OUT_ROW_PARTS),
        grid=(batch, nt),
        in_specs=[
            pl.BlockSpec(tile(GLA_V), at(0)),
            pl.BlockSpec(tile(SB_W), at(0)),
            pl.BlockSpec(tile(hw), at(COL_MG)),
            pl.BlockSpec(tile(hw), at(COL_MG + hw)),
            pl.BlockSpec(tile(hw), at(COL_MS)),
            pl.BlockSpec(tile(hw), at(COL_MS + hw)),
            pl.BlockSpec(tile(cm), at(0)),
            resident((GLA_V, cm)),
            resident((SB_W, cm)),
            resident((cm, cm)),
            pl.BlockSpec((1, cm), const),
        ],
        out_specs=pl.BlockSpec((tm, cm), lambda b, i: (b * nt + i, 0)),
        out_shape=jax.ShapeDtypeStruct((batch * rows, cm), F32),
        compiler_params=pltpu.CompilerParams(
            dimension_semantics=("arbitrary", "arbitrary"), vmem_limit_bytes=VMEM_LIMIT),
    )(og, osb, p, p, p, p, x, w_up_gla, w_up_sb, w_o, post_gain).reshape(batch, rows, cm)


def _out_proj(og, osb, p, x, w_up_gla, w_up_sb, w_o, post_gain, *, row_target):
    n = x.shape[0]
    tm = _row_tile(n, row_target)
    const = lambda i: (0, 0)
    cm = D_MODEL
    hw = cm // 2
    half = lambda c: pl.BlockSpec((tm, hw), lambda i: (i, c // hw))
    resident = lambda shape: pl.BlockSpec(shape, const, pipeline_mode=pl.Buffered(1))
    return pl.pallas_call(
        functools.partial(_out_proj_kernel, parts=OUT_ROW_PARTS),
        grid=(n // tm,),
        in_specs=[
            pl.BlockSpec((tm, GLA_V), lambda i: (i, 0)),
            pl.BlockSpec((tm, SB_W), lambda i: (i, 0)),
            half(COL_MG), half(COL_MG + hw), half(COL_MS), half(COL_MS + hw),
            pl.BlockSpec((tm, cm), lambda i: (i, 0)),
            resident((GLA_V, cm)),
            resident((SB_W, cm)),
            resident((cm, cm)),
            pl.BlockSpec((1, cm), const),
        ],
        out_specs=pl.BlockSpec((tm, cm), lambda i: (i, 0)),
        out_shape=jax.ShapeDtypeStruct((n, cm), F32),
        compiler_params=pltpu.CompilerParams(
            dimension_semantics=("arbitrary",), vmem_limit_bytes=VMEM_LIMIT),
    )(og, osb, p, p, p, p, x, w_up_gla, w_up_sb, w_o, post_gain)


def _pack_weights(pre_gain, w_in, w_a2, b_a, gla_gain, w_up_gla, w_up_sb, w_o, post_gain):
    sizes = (GLA_QK, GLA_QK, GLA_V, GLA_V, GLA_RANK, SB_W, SB_W, SB_W, SB_W, D_MODEL, D_MODEL)
    offs = [0]
    for s in sizes:
        offs.append(offs[-1] + s)
    r0, r1 = offs[4], offs[5]
    w_main = jnp.concatenate([w_in[:, :r0], w_in[:, r1:]], axis=1).astype(BF16)
    w_r = jnp.pad(w_in[:, r0:r1], ((0, 0), (0, LANES - GLA_RANK))).astype(BF16)
    w_a2p = jnp.pad(w_a2.astype(F32), ((0, LANES - GLA_RANK), (0, 0)))
    return dict(
        pre_gain=pre_gain.reshape(1, D_MODEL).astype(F32), w_main=w_main, w_r=w_r, w_a2p=w_a2p,
        b_a=b_a.reshape(1, GLA_QK).astype(F32), gla_gain=gla_gain.astype(F32),
        w_up_gla=w_up_gla.astype(BF16), w_up_sb=w_up_sb.astype(BF16), w_o=w_o.astype(BF16),
        post_gain=post_gain.reshape(1, D_MODEL).astype(F32))


def _layer(x, batch, past, s0, w, kv_prev, *, layer, depth, in_rows, out_rows, drop_lead=0):
    n = x.shape[0]
    T = n // batch
    p, k_all, v_all, la, la_min = _in_proj(
        x, w["pre_gain"], w["w_main"], w["w_r"], w["w_a2p"], w["b_a"], kv_prev,
        layer=layer, depth=depth, row_target=in_rows)
    safe = (jnp.min(la_min) >= GLA_SAFE_LOG_DECAY).astype(jnp.int32).reshape(1)
    p3 = p.reshape(batch, T, P_COLS)
    og, s_new = _gla(p3, la.reshape(batch, T, GLA_QK), s0, w["gla_gain"], safe)
    if past is None:
        osb = _sb_prompt(p3)
    else:
        osb = _sb_sample(p3, past[0], past[1], layer=layer)
    tail_w = (w["w_up_gla"], w["w_up_sb"], w["w_o"], w["post_gain"])
    og, osb = og.reshape(n, GLA_V), osb.reshape(n, SB_W)
    if drop_lead:
        y = _out_proj_tail(og, osb, p, x, *tail_w, batch=batch, lead=drop_lead, row_target=out_rows)
    else:
        y = _out_proj(og, osb, p, x, *tail_w, row_target=out_rows)
    return y, (k_all, v_all), s_new


def kernel(x_prompt, x_sample, cache_sb_k, cache_sb_v, state_gla, meta_tokens, pre_gain, w_in, w_a2, b_a,
           gla_gain, w_up_gla, w_up_sb, w_o, post_gain):
    B, seq, _ = x_prompt.shape
    Bs, seq_s, _ = x_sample.shape
    depth = w_in.shape[0]
    T = N_META + seq
    meta = jnp.broadcast_to(meta_tokens.astype(x_prompt.dtype)[None], (B, N_META, D_MODEL))
    hp = jnp.concatenate([meta, x_prompt], axis=1).reshape(B * T, D_MODEL)
    hs = x_sample.reshape(Bs * seq_s, D_MODEL)
    zero_state = jnp.zeros((B, GLA_HEADS, GLA_DK, GLA_DV), F32)
    cache_rows = (depth, Bs, cache_sb_k.shape[2] * SB_HEADS, SB_HEAD_DIM)
    cache_k2, cache_v2 = cache_sb_k.reshape(cache_rows), cache_sb_v.reshape(cache_rows)
    kv_p, kv_s, sp, ss_ = None, None, [], []
    for l in range(depth):
        w = _pack_weights(pre_gain[l], w_in[l], w_a2[l], b_a[l], gla_gain[l], w_up_gla[l], w_up_sb[l],
                          w_o[l], post_gain[l])
        hp, kv_p, s_ = _layer(hp, B, None, zero_state, w, kv_p, layer=l, depth=depth, in_rows=768,
                              out_rows=512, drop_lead=N_META if l == depth - 1 else 0)
        sp.append(s_)
        hs, kv_s, s_ = _layer(hs, Bs, (cache_k2, cache_v2), state_gla[l], w, kv_s,
                              layer=l, depth=depth, in_rows=512, out_rows=256)
        ss_.append(s_)
    y_prompt = hp
    y_sample = hs.reshape(Bs, seq_s, D_MODEL)
    shape_p = (depth, B, T, SB_HEADS, SB_HEAD_DIM)
    shape_s = (depth, Bs, seq_s, SB_HEADS, SB_HEAD_DIM)
    return (y_prompt, y_sample, kv_p[0].reshape(shape_p), kv_p[1].reshape(shape_p), jnp.stack(sp),
            kv_s[0].reshape(shape_s), kv_s[1].reshape(shape_s), jnp.stack(ss_))
```

```python
import functools

import jax
import jax.numpy as jnp
from jax import lax
from jax.experimental import pallas as pl
from jax.experimental.pallas import tpu as pltpu

F32 = jnp.float32
BF16 = jnp.bfloat16

D_MODEL = 2048
N_META = 16
EPS = 1e-6
GLA_HEADS = 4
GLA_DK = 128
GLA_DV = 256
GLA_RANK = 16
GLA_TAU = 16.0
SB_HEADS = 8
SB_HEAD_DIM = 128
GLA_QK = GLA_HEADS * GLA_DK
GLA_V = GLA_HEADS * GLA_DV
SB_W = SB_HEADS * SB_HEAD_DIM

LANES = 128
PROJ_TILE = 1024
P_COLS = 11 * PROJ_TILE
COL_GQ, COL_GK, COL_GV, COL_GZ, COL_SQ, COL_SK, COL_SV, COL_SZ, COL_MG, COL_MS = (
    0, 512, 1024, 2048, 3072, 4096, 5120, 6144, 7168, 9216)
OUT_ROW_PARTS = 2

GLA_CHUNK = 64
GLA_HEADS_PER_STEP = 4
GLA_SAFE_LOG_DECAY = -1.0
SB_BLOCK = 128
SB_QUERY_TILE = 64
SB_WINDOW = 256
SB_TILE_GROUP = 4
SB_HEADS_PER_STEP = 4
SB_STATIC_BLOCKS = 2
SB_STOP_LOG = -120.0
VMEM_LIMIT = 56 * 1024 * 1024


def _row_tile(n, target):
    best = None
    for d in range(16, min(n, target) + 1, 16):
        if n % d == 0:
            best = d
    return best if best is not None else n


def _sigmoid(x):
    return 1.0 / (1.0 + jnp.exp(-x))


def _log_sigmoid(x):
    return jnp.minimum(x, 0.0) - jnp.log(1.0 + jnp.exp(-jnp.abs(x)))


def _dot(a, b):
    return jnp.dot(a, b, preferred_element_type=F32)


def _dot_nt(a, b):
    return lax.dot_general(a, b, (((1,), (1,)), ((), ())), preferred_element_type=F32)


def _dot_tn(a, b):
    return lax.dot_general(a, b, (((0,), (0,)), ((), ())), preferred_element_type=F32)


def _split3(x):
    hi = x.astype(BF16)
    r = x - hi.astype(F32)
    mid = r.astype(BF16)
    lo = (r - mid.astype(F32)).astype(BF16)
    return hi, mid, lo


def _in_proj_kernel(x_ref, g_ref, w_ref, wr_ref, wa2_ref, ba_ref, *rest, sub, aliased):
    if aliased:
        rest = rest[2:]
    p_ref, k_ref, v_ref, la_ref, lamin_ref, h_ref = rest
    j = pl.program_id(1)
    tm = x_ref.shape[0]

    @pl.when(j == 0)
    def _():
        gain = g_ref[...]

        def normalize(r0, n):
            for s in range(r0, r0 + n, sub):
                x = x_ref[s:s + sub, :]
                ms = jnp.mean(x * x, axis=-1, keepdims=True)
                h_ref[s:s + sub, :] = (x * lax.rsqrt(ms + EPS) * gain).astype(BF16)

        parts = 3 if tm % (3 * sub) == 0 else 1
        rp = tm // parts
        groups = [pl.ds(i * rp, rp) for i in range(parts)]
        g_rs = []
        for i, r in enumerate(groups):
            normalize(i * rp, rp)
            g_rs.append(_dot(h_ref[r, :], wr_ref[...]))
        w2h, w2m, _ = _split3(wa2_ref[...])
        pieces = [_split3(g_r)[:2] for g_r in g_rs]
        xas = [_dot(hi, w2h) + _dot(hi, w2m) + _dot(mid, w2h) + ba_ref[...] for hi, mid in pieces]
        las = [_log_sigmoid(xa) * (1.0 / GLA_TAU) for xa in xas]
        for r, la in zip(groups, las):
            la_ref[r, :] = la
        low = functools.reduce(jnp.minimum, las)
        m = jnp.min(jnp.min(low, axis=0, keepdims=True), axis=1, keepdims=True)
        lamin_ref[...] = jnp.broadcast_to(m.reshape(1, 1, 1), lamin_ref.shape)

    acc = _dot(h_ref[...], w_ref[...])
    p_ref[...] = acc.astype(BF16)

    def heads_out(ref):
        for h in range(SB_HEADS):
            ref[pl.ds(h, tm, stride=SB_HEADS), :] = acc[:, h * SB_HEAD_DIM:(h + 1) * SB_HEAD_DIM]

    @pl.when(j == COL_SK // PROJ_TILE)
    def _():
        heads_out(k_ref)

    @pl.when(j == COL_SV // PROJ_TILE)
    def _():
        heads_out(v_ref)


def _in_proj(x, gain, w_main, w_r, w_a2p, b_a, kv_prev, *, layer, depth, row_target):
    n = x.shape[0]
    tm = _row_tile(n, row_target)
    nt = n // tm
    sub = _row_tile(tm, 64)
    grid = (nt, P_COLS // PROJ_TILE)
    const = lambda i, j: (0, 0)
    aliased = kv_prev is not None
    kv_spec = pl.BlockSpec((None, tm * SB_HEADS, SB_HEAD_DIM), lambda i, j: (layer, i, 0))
    kv_shape = jax.ShapeDtypeStruct((depth, n * SB_HEADS, SB_HEAD_DIM), F32)
    in_specs = [
        pl.BlockSpec((tm, D_MODEL), lambda i, j: (i, 0)),
        pl.BlockSpec((1, D_MODEL), const),
        pl.BlockSpec((D_MODEL, PROJ_TILE), lambda i, j: (0, j)),
        pl.BlockSpec((D_MODEL, LANES), const),
        pl.BlockSpec((LANES, GLA_QK), const),
        pl.BlockSpec((1, GLA_QK), const),
    ]
    args = [x, gain, w_main, w_r, w_a2p, b_a]
    aliases = {}
    if aliased:
        in_specs += [pl.BlockSpec(memory_space=pl.ANY), pl.BlockSpec(memory_space=pl.ANY)]
        args += list(kv_prev)
        aliases = {6: 1, 7: 2}
    return pl.pallas_call(
        functools.partial(_in_proj_kernel, sub=sub, aliased=aliased),
        grid=grid,
        in_specs=in_specs,
        out_specs=[
            pl.BlockSpec((tm, PROJ_TILE), lambda i, j: (i, j)),
            kv_spec,
            kv_spec,
            pl.BlockSpec((tm, GLA_QK), lambda i, j: (i, 0)),
            pl.BlockSpec((1, 8, LANES), lambda i, j: (i, 0, 0)),
        ],
        out_shape=[
            jax.ShapeDtypeStruct((n, P_COLS), BF16),
            kv_shape,
            kv_shape,
            jax.ShapeDtypeStruct((n, GLA_QK), F32),
            jax.ShapeDtypeStruct((nt, 8, LANES), F32),
        ],
        scratch_shapes=[pltpu.VMEM((tm, D_MODEL), BF16)],
        input_output_aliases=aliases,
        compiler_params=pltpu.CompilerParams(
            dimension_semantics=("arbitrary", "arbitrary"), vmem_limit_bytes=VMEM_LIMIT),
    )(*args)


def _gla_kernel(safe_ref, q_ref, k_ref, v_ref, z_ref, la_ref, s0_ref, gain_ref,
                o_ref, sout_ref, st_ref, oi_ref, qs_ref, bs_ref, *, chunk, hp):
    C = chunk
    T = q_ref.shape[0]
    n_full, rem = T // C, T % C
    safe = safe_ref[0] > 0

    row = lax.broadcasted_iota(jnp.int32, (C, C), 0)
    col = lax.broadcasted_iota(jnp.int32, (C, C), 1)
    causal = row >= col
    tri = jnp.where(causal, 1.0, 0.0).astype(BF16)
    tri3 = jnp.concatenate([tri, tri, tri], axis=1)
    row1 = lax.broadcasted_iota(jnp.int32, (C, 1), 0)

    for hh in range(hp):
        st_ref[hh] = s0_ref[hh].T

    def group(chunks, fast):
        items = [(hh, off, pad) for off, pad in chunks for hh in range(hp)]
        kcols = lambda hh: slice(hh * GLA_DK, (hh + 1) * GLA_DK)
        vcols = lambda hh: slice(hh * GLA_DV, (hh + 1) * GLA_DV)
        qs, ks, vs, las = [], [], [], []
        for hh, off, pad in items:
            rows = pl.ds(off, C)
            q = q_ref[rows, kcols(hh)].astype(F32) * (GLA_DK ** -0.5)
            k = k_ref[rows, kcols(hh)].astype(F32)
            v = v_ref[rows, vcols(hh)]
            la = la_ref[rows, kcols(hh)]
            if pad:
                valid = row1 >= pad
                la = jnp.where(valid, la, 0.0)
                k = jnp.where(valid, k, 0.0)
                v = jnp.where(valid, v, jnp.zeros_like(v))
            qs.append(q)
            ks.append(k)
            vs.append(v)
            las.append(jnp.concatenate(_split3(la), axis=0))
        bs = [_dot(tri3, la3) for la3 in las]
        qes, kds, ends = [], [], []
        for q, k, b in zip(qs, ks, bs):
            b_end = b[C - 1:C, :]
            qes.append((q * jnp.exp(b)).astype(BF16))
            kds.append((k * jnp.exp(b_end - b)).astype(BF16))
            ends.append(jnp.exp(b_end))
        if fast:
            kes = [(k * jnp.exp(-b)).astype(BF16) for k, b in zip(ks, bs)]
            ss = [jnp.where(causal, _dot_nt(qe, ke), 0.0).astype(BF16) for qe, ke in zip(qes, kes)]
            intra = [_dot(s, v) for s, v in zip(ss, vs)]
        else:
            intra = []
            for q, k, v, b in zip(qs, ks, vs, bs):
                qs_ref[...] = q
                bs_ref[...] = b
                vf = v.astype(F32)

                def body(t, carry, k=k, b=b, vf=vf):
                    bt = bs_ref[pl.ds(t, 1), :]
                    qt = qs_ref[pl.ds(t, 1), :]
                    diff = jnp.where(row1 <= t, bt - b, -jnp.inf)
                    w = jnp.sum(k * jnp.exp(diff) * qt, axis=-1, keepdims=True)
                    oi_ref[pl.ds(t, 1), :] = jnp.sum(w * vf, axis=0, keepdims=True)
                    return carry

                lax.fori_loop(0, C, body, 0)
                intra.append(oi_ref[...])
        kvs = [_dot_tn(v, kd) for v, kd in zip(vs, kds)]
        sts = [st_ref[hh] for hh in range(hp)]
        os_ = []
        for i, (hh, off, pad) in enumerate(items):
            os_.append(_dot_nt(qes[i], sts[hh].astype(BF16)) + intra[i])
            sts[hh] = sts[hh] * ends[i] + kvs[i]
        for hh in range(hp):
            st_ref[hh] = sts[hh]
        for (hh, off, pad), o in zip(items, os_):
            on = o * lax.rsqrt(jnp.mean(o * o, axis=-1, keepdims=True) + EPS) * gain_ref[hh]
            z = z_ref[pl.ds(off, C), vcols(hh)].astype(F32)
            out = (on * (z * _sigmoid(z))).astype(BF16)
            if pad:
                o_ref[pl.ds(off + pad, C - pad), vcols(hh)] = out[pad:, :]
            else:
                o_ref[pl.ds(off, C), vcols(hh)] = out

    def run(fast):
        per = max(c for c in (1, 2, 4) if n_full % c == 0)

        def body(c, carry):
            base = pl.multiple_of(c * (per * C), per * C)
            group([(base + i * C, 0) for i in range(per)], fast)
            return carry

        lax.fori_loop(0, n_full // per, body, 0)
        if rem:
            group([(T - C, C - rem)], fast)

    pl.when(safe)(lambda: run(True))
    pl.when(jnp.logical_not(safe))(lambda: run(False))
    for hh in range(hp):
        sout_ref[hh] = st_ref[hh].T


def _gla(p3, la3, s0, gla_gain, safe, *, chunk=GLA_CHUNK, hp=GLA_HEADS_PER_STEP):
    B, T, _ = p3.shape
    H = GLA_HEADS
    wk, wv = hp * GLA_DK, hp * GLA_DV
    kq, kk = COL_GQ // wk, COL_GK // wk
    kv, kz = COL_GV // wv, COL_GZ // wv
    grid_spec = pltpu.PrefetchScalarGridSpec(
        num_scalar_prefetch=1,
        grid=(B, H // hp),
        in_specs=[
            pl.BlockSpec((None, T, wk), lambda b, g, s: (b, 0, kq + g)),
            pl.BlockSpec((None, T, wk), lambda b, g, s: (b, 0, kk + g)),
            pl.BlockSpec((None, T, wv), lambda b, g, s: (b, 0, kv + g)),
            pl.BlockSpec((None, T, wv), lambda b, g, s: (b, 0, kz + g)),
            pl.BlockSpec((None, T, wk), lambda b, g, s: (b, 0, g)),
            pl.BlockSpec((None, hp, GLA_DK, GLA_DV), lambda b, g, s: (b, g, 0, 0)),
            pl.BlockSpec((hp, 1, GLA_DV), lambda b, g, s: (g, 0, 0)),
        ],
        out_specs=[
            pl.BlockSpec((None, T, wv), lambda b, g, s: (b, 0, g)),
            pl.BlockSpec((None, hp, GLA_DK, GLA_DV), lambda b, g, s: (b, g, 0, 0)),
        ],
        scratch_shapes=[
            pltpu.VMEM((hp, GLA_DV, GLA_DK), F32),
            pltpu.VMEM((chunk, GLA_DV), F32),
            pltpu.VMEM((chunk, GLA_DK), F32),
            pltpu.VMEM((chunk, GLA_DK), F32),
        ],
    )
    return pl.pallas_call(
        functools.partial(_gla_kernel, chunk=chunk, hp=hp),
        grid_spec=grid_spec,
        out_shape=[
            jax.ShapeDtypeStruct((B, T, GLA_V), BF16),
            jax.ShapeDtypeStruct((B, H, GLA_DK, GLA_DV), F32),
        ],
        compiler_params=pltpu.CompilerParams(
            dimension_semantics=("arbitrary", "arbitrary"), vmem_limit_bytes=VMEM_LIMIT),
    )(safe, p3, p3, p3, p3, la3, s0, gla_gain.reshape(H, 1, GLA_DV))


def _strict_upper(n):
    j = lax.broadcasted_iota(jnp.int32, (n, n), 0)
    s = lax.broadcasted_iota(jnp.int32, (n, n), 1)
    return jnp.where(j > s, 1.0, 0.0).astype(BF16)


def _sb_step(q, kb, vb, upper, mask, carry, acc):
    z = _dot_nt(q, kb)
    lb = _log_sigmoid(z)
    lom = lb - z
    if mask is not None:
        lom = jnp.where(mask, lom, 0.0)
    hi = lom.astype(BF16)
    lo = (lom - hi.astype(F32)).astype(BF16)
    tail = _dot(hi, upper) + _dot(lo, upper) + carry
    w = jnp.exp(lb + tail)
    if mask is not None:
        w = jnp.where(mask, w, 0.0)
    acc = acc + _dot(w.astype(BF16), vb)
    carry = carry + jnp.sum(lom, axis=-1, keepdims=True)
    return carry, acc


def _sb_prompt_kernel(q_ref, k_ref, v_ref, z_ref, o_ref, c_ref, a_ref, *, blk, tq, win, hp, group):
    T = q_ref.shape[0]
    d = SB_HEAD_DIM
    back = win - tq
    n_tiles, rem = T // tq, T % tq
    upper = _strict_upper(blk)
    upper2 = jnp.concatenate([upper, upper], axis=0)
    rowq = lax.broadcasted_iota(jnp.int32, (tq, blk), 0)
    colk = lax.broadcasted_iota(jnp.int32, (tq, blk), 1)
    scale = SB_HEAD_DIM ** -0.5
    heads = [slice(hh * d, (hh + 1) * d) for hh in range(hp)]

    def block_mask(delta, a):
        if (a + 1) * blk - 1 < delta:
            return None
        if a * blk >= delta + tq - 1:
            return False
        return a * blk + colk < delta + rowq

    def tiles(specs):
        units = [(r0, k0, delta, cols) for r0, k0, delta in specs for cols in heads]
        masks = [[m for m in (block_mask(delta, a) for a in range(win // blk)) if m is not False]
                 for _, _, delta, _ in units]
        krows = [pl.ds(k0, len(ms) * blk) for (_, k0, _, _), ms in zip(units, masks)]
        qs = [(q_ref[pl.ds(r0, tq), cols].astype(F32) * scale).astype(BF16) for r0, _, _, cols in units]
        zs = [_dot_nt(q, k_ref[kr, cols]) for q, kr, (_, _, _, cols) in zip(qs, krows, units)]
        lbs, hls, sums = [], [], []
        for z, ms in zip(zs, masks):
            lb = _log_sigmoid(z)
            lom = lb - z
            hl, rs = [], []
            for a, m in enumerate(ms):
                lom_a = lom[:, a * blk:(a + 1) * blk]
                if m is not None:
                    lom_a = jnp.where(m, lom_a, 0.0)
                hi = lom_a.astype(BF16)
                lo = (lom_a - hi.astype(F32)).astype(BF16)
                hl.append(jnp.concatenate([hi, lo], axis=1))
                rs.append(jnp.sum(lom_a, axis=-1, keepdims=True))
            lbs.append(lb)
            hls.append(hl)
            sums.append(rs)
        tails = [[_dot(hl_a, upper2) for hl_a in hl] for hl in hls]
        ws, carries = [], []
        for lb, tl, rs, ms in zip(lbs, tails, sums, masks):
            carry = jnp.zeros((tq, 1), F32)
            w = [None] * len(ms)
            for a in range(len(ms) - 1, -1, -1):
                w_a = jnp.exp(lb[:, a * blk:(a + 1) * blk] + tl[a] + carry)
                if ms[a] is not None:
                    w_a = jnp.where(ms[a], w_a, 0.0)
                w[a] = w_a.astype(BF16)
                carry = carry + rs[a]
            ws.append(jnp.concatenate(w, axis=1) if len(w) > 1 else w[0])
            carries.append(carry)
        accs = [_dot(w, v_ref[kr, cols]) for w, kr, (_, _, _, cols) in zip(ws, krows, units)]

        def emit(u, acc):
            r0, _, _, cols = units[u]
            zg = z_ref[pl.ds(r0, tq), cols].astype(F32)
            o_ref[pl.ds(r0, tq), cols] = (acc * (zg * _sigmoid(zg))).astype(BF16)

        for u in range(len(units)):
            emit(u, accs[u])
        if all(isinstance(k0, int) and k0 == 0 for _, k0, _, _ in units):
            return
        worst = functools.reduce(jnp.maximum, carries)

        @pl.when(jnp.max(worst) > SB_STOP_LOG)
        def _():
            for u in range(len(units)):
                c_ref[u] = carries[u]
                a_ref[u] = accs[u]
            for u, (r0, k0, _, cols) in enumerate(units):
                n_prev = k0 // blk
                head = k0 - n_prev * blk
                q = qs[u]

                def cond(s):
                    return jnp.logical_and(s[0] < n_prev, s[1] > SB_STOP_LOG)

                def body(s, u=u, cols=cols, q=q, k0=k0):
                    ks = pl.ds(pl.multiple_of(k0 - (s[0] + 1) * blk, 16), blk)
                    c, a = _sb_step(q, k_ref[ks, cols], v_ref[ks, cols], upper, None, c_ref[u], a_ref[u])
                    c_ref[u] = c
                    a_ref[u] = a
                    return s[0] + 1, jnp.max(c)

                _, cmax = lax.while_loop(cond, body, (jnp.int32(0), jnp.max(carries[u])))
                if not (isinstance(head, int) and head == 0):
                    @pl.when(jnp.logical_and(cmax > SB_STOP_LOG, head > 0))
                    def _(u=u, cols=cols, q=q, head=head):
                        _, a = _sb_step(q, k_ref[0:blk, cols], v_ref[0:blk, cols], upper, colk < head,
                                        c_ref[u], a_ref[u])
                        a_ref[u] = a
                emit(u, a_ref[u])

    n_lead = min(n_tiles, -(-back // tq))
    static = [(i * tq, 0, i * tq) for i in range(n_lead)]
    first = n_lead + (n_tiles - n_lead) % group
    static += [(i * tq, i * tq - back, back) for i in range(n_lead, first)]
    if rem:
        static.append((T - tq, T - tq - back, back))
    for g in range(0, len(static), group):
        tiles(static[g:g + group])

    def body(m, carry):
        r0s = [pl.multiple_of((first + m * group + g) * tq, tq) for g in range(group)]
        tiles([(r0, r0 - back, back) for r0 in r0s])
        return carry

    lax.fori_loop(0, (n_tiles - first) // group, body, 0)


def _sb_prompt(p3, *, blk=SB_BLOCK, tq=SB_QUERY_TILE, win=SB_WINDOW, hp=SB_HEADS_PER_STEP, group=SB_TILE_GROUP):
    B, T, _ = p3.shape
    assert T >= win and win % blk == 0 and tq <= blk and tq % 16 == 0
    w = hp * SB_HEAD_DIM
    spec = lambda c: pl.BlockSpec((None, T, w), lambda b, g: (b, 0, c // w + g))
    return pl.pallas_call(
        functools.partial(_sb_prompt_kernel, blk=blk, tq=tq, win=win, hp=hp, group=group),
        grid=(B, SB_HEADS // hp),
        in_specs=[spec(COL_SQ), spec(COL_SK), spec(COL_SV), spec(COL_SZ)],
        out_specs=pl.BlockSpec((None, T, w), lambda b, g: (b, 0, g)),
        out_shape=jax.ShapeDtypeStruct((B, T, SB_W), BF16),
        scratch_shapes=[pltpu.VMEM((group * hp, tq, 1), F32), pltpu.VMEM((group * hp, tq, SB_HEAD_DIM), F32)],
        compiler_params=pltpu.CompilerParams(
            dimension_semantics=("arbitrary", "arbitrary"), vmem_limit_bytes=VMEM_LIMIT),
    )(p3, p3, p3, p3)


def _sb_sample_kernel(q_ref, k_ref, v_ref, z_ref, pk_ref, pv_ref, pk_hbm, pv_hbm, o_ref,
                      kbuf, vbuf, c_ref, a_ref, sem, *, blk, layer):
    b = pl.program_id(0)
    tq = q_ref.shape[0]
    d = SB_HEAD_DIM
    nh = SB_HEADS
    tail_len = pk_ref.shape[0] // nh
    n_stat = tail_len // blk
    past_len = pk_hbm.shape[2] // nh
    n_past = past_len // blk
    row = lax.broadcasted_iota(jnp.int32, (tq, tq), 0)
    col = lax.broadcasted_iota(jnp.int32, (tq, tq), 1)
    upper_q = _strict_upper(tq)
    upper = _strict_upper(blk)
    scale = SB_HEAD_DIM ** -0.5

    def fetch(s0):
        src = pl.ds(pl.multiple_of(s0 * nh, blk * nh), blk * nh)
        ck = pltpu.make_async_copy(pk_hbm.at[layer, b, src], kbuf, sem.at[0])
        cv = pltpu.make_async_copy(pv_hbm.at[layer, b, src], vbuf, sem.at[1])
        ck.start()
        cv.start()
        ck.wait()
        cv.wait()

    heads = [slice(h * d, (h + 1) * d) for h in range(SB_HEADS)]
    strict = col < row
    nb = n_stat + 1
    past_rows = lambda h, a: pl.ds((tail_len - (n_stat - a) * blk) * nh + h, blk, stride=nh)
    kbs = [[pk_ref[past_rows(h, a), :].astype(BF16) for a in range(n_stat)] + [k_ref[:, cols]]
           for h, cols in enumerate(heads)]
    vbs = [[pv_ref[past_rows(h, a), :].astype(BF16) for a in range(n_stat)] + [v_ref[:, cols]]
           for h, cols in enumerate(heads)]
    uppers = [jnp.concatenate([upper, upper], axis=0)] * n_stat + [jnp.concatenate([upper_q, upper_q], axis=0)]
    qs = [(q_ref[:, cols].astype(F32) * scale).astype(BF16) for cols in heads]
    zs = [[_dot_nt(q, kb) for kb in kb_h] for q, kb_h in zip(qs, kbs)]
    lbs, hls, sums = [], [], []
    for z_h in zs:
        lb_h, hl_h, rs_h = [], [], []
        for a, z in enumerate(z_h):
            lb = _log_sigmoid(z)
            lom = lb - z
            if a == nb - 1:
                lom = jnp.where(strict, lom, 0.0)
            hi = lom.astype(BF16)
            lo = (lom - hi.astype(F32)).astype(BF16)
            lb_h.append(lb)
            hl_h.append(jnp.concatenate([hi, lo], axis=1))
            rs_h.append(jnp.sum(lom, axis=-1, keepdims=True))
        lbs.append(lb_h)
        hls.append(hl_h)
        sums.append(rs_h)
    tails = [[_dot(hl, up) for hl, up in zip(hl_h, uppers)] for hl_h in hls]
    ws, carries = [], []
    for lb_h, tl_h, rs_h in zip(lbs, tails, sums):
        carry = jnp.zeros((tq, 1), F32)
        w_h = [None] * nb
        for a in range(nb - 1, -1, -1):
            w = jnp.exp(lb_h[a] + tl_h[a] + carry)
            if a == nb - 1:
                w = jnp.where(strict, w, 0.0)
            w_h[a] = w.astype(BF16)
            carry = carry + rs_h[a]
        ws.append(w_h)
        carries.append(carry)
    accs = [functools.reduce(jnp.add, [_dot(w, vb) for w, vb in zip(w_h, vb_h)]) for w_h, vb_h in zip(ws, vbs)]

    def emit(h, acc):
        zg = z_ref[:, heads[h]].astype(F32)
        o_ref[:, heads[h]] = (acc * (zg * _sigmoid(zg))).astype(BF16)

    for h in range(SB_HEADS):
        emit(h, accs[h])
    if n_past == n_stat:
        return
    worst = functools.reduce(jnp.maximum, carries)

    @pl.when(jnp.max(worst) > SB_STOP_LOG)
    def _():
        for h in range(SB_HEADS):
            c_ref[h] = carries[h]
            a_ref[h] = accs[h]
        for h in range(SB_HEADS):
            def cond(s):
                return jnp.logical_and(s[0] < n_past, s[1] > SB_STOP_LOG)

            def body(s, h=h, q=qs[h]):
                fetch(pl.multiple_of(past_len - (s[0] + 1) * blk, blk))
                hs = pl.ds(h, blk, stride=nh)
                c, a = _sb_step(q, kbuf[hs, :].astype(BF16), vbuf[hs, :].astype(BF16), upper, None,
                                c_ref[h], a_ref[h])
                c_ref[h] = c
                a_ref[h] = a
                return s[0] + 1, jnp.max(c)

            lax.while_loop(cond, body, (jnp.int32(n_stat), jnp.max(carries[h])))
            emit(h, a_ref[h])


def _sb_sample(p3, past_k, past_v, *, layer, blk=SB_BLOCK, n_static=SB_STATIC_BLOCKS):
    B, T, _ = p3.shape
    nh = SB_HEADS
    past_len = past_k.shape[2] // nh
    assert past_len % blk == 0 and T % 16 == 0
    n_static = min(n_static, past_len // blk)
    tail = n_static * blk
    assert past_len % tail == 0
    spec = lambda c: pl.BlockSpec((None, T, SB_W), lambda b: (b, 0, c // SB_W))
    past_tail = pl.BlockSpec((None, None, tail * nh, SB_HEAD_DIM),
                             lambda b: (layer, b, past_len // tail - 1, 0))
    hbm = pl.BlockSpec(memory_space=pl.ANY)
    return pl.pallas_call(
        functools.partial(_sb_sample_kernel, blk=blk, layer=layer),
        grid=(B,),
        in_specs=[spec(COL_SQ), spec(COL_SK), spec(COL_SV), spec(COL_SZ), past_tail, past_tail, hbm, hbm],
        out_specs=pl.BlockSpec((None, T, SB_W), lambda b: (b, 0, 0)),
        out_shape=jax.ShapeDtypeStruct((B, T, SB_W), BF16),
        scratch_shapes=[
            pltpu.VMEM((blk * nh, SB_HEAD_DIM), F32),
            pltpu.VMEM((blk * nh, SB_HEAD_DIM), F32),
            pltpu.VMEM((SB_HEADS, T, 1), F32),
            pltpu.VMEM((SB_HEADS, T, SB_HEAD_DIM), F32),
            pltpu.SemaphoreType.DMA((2,)),
        ],
        compiler_params=pltpu.CompilerParams(
            dimension_semantics=("arbitrary",), vmem_limit_bytes=VMEM_LIMIT),
    )(p3, p3, p3, p3, past_k, past_v, past_k, past_v)


def _out_proj_kernel(og_ref, os_ref, mg0_ref, mg1_ref, ms0_ref, ms1_ref, x_ref, wg_ref, ws_ref, wo_ref,
                     g_ref, y_ref, *, parts):
    rp = x_ref.shape[0] // parts
    groups = [pl.ds(i * rp, rp) for i in range(parts)]
    hw = D_MODEL // 2
    gate = lambda ref, r: _sigmoid(ref[r, :].astype(F32))
    y_gla = [_dot(og_ref[r, :], wg_ref[...]) for r in groups]
    y_sb = [_dot(os_ref[r, :], ws_ref[...]) for r in groups]
    merged = []
    for r, yg, ys in zip(groups, y_gla, y_sb):
        lo = gate(mg0_ref, r) * yg[:, :hw] + gate(ms0_ref, r) * ys[:, :hw]
        hi = gate(mg1_ref, r) * yg[:, hw:] + gate(ms1_ref, r) * ys[:, hw:]
        merged.append(jnp.concatenate([lo, hi], axis=1).astype(BF16))
    us = [_dot(m, wo_ref[...]) for m in merged]
    for r, u in zip(groups, us):
        un = u * lax.rsqrt(jnp.mean(u * u, axis=-1, keepdims=True) + EPS) * g_ref[...]
        y_ref[r, :] = x_ref[r, :] + un


def _out_proj_tail(og, osb, p, x, w_up_gla, w_up_sb, w_o, post_gain, *, batch, lead, row_target):
    T = x.shape[0] // batch
    rows = T - lead
    tm = _row_tile(rows, row_target)
    nt = rows // tm
    const = lambda b, i: (0, 0)
    cm = D_MODEL
    assert T % 16 == 0 and lead % 16 == 0 and tm % 16 == 0
    at = lambda col: (lambda b, i: (pl.multiple_of(b * T + lead + i * tm, 16), col))
    resident = lambda shape: pl.BlockSpec(shape, const, pipeline_mode=pl.Buffered(1))
    tile = lambda cols: (pl.Element(tm), pl.Element(cols))
    hw = cm // 2
    return pl.pallas_call(
        functools.partial(_out_proj_kernel, parts=OUT_ROW_PARTS),
        grid=(batch, nt),
        in_specs=[
            pl.BlockSpec(tile(GLA_V), at(0)),
            pl.BlockSpec(tile(SB_W), at(0)),
            pl.BlockSpec(tile(hw), at(COL_MG)),
            pl.BlockSpec(tile(hw), at(COL_MG + hw)),
            pl.BlockSpec(tile(hw), at(COL_MS)),
            pl.BlockSpec(tile(hw), at(COL_MS + hw)),
            pl.BlockSpec(tile(cm), at(0)),
            resident((GLA_V, cm)),
            resident((SB_W, cm)),
            resident((cm, cm)),
            pl.BlockSpec((1, cm), const),
        ],
        out_specs=pl.BlockSpec((tm, cm), lambda b, i: (b * nt + i, 0)),
        out_shape=jax.ShapeDtypeStruct((batch * rows, cm), F32),
        compiler_params=pltpu.CompilerParams(
            dimension_semantics=("arbitrary", "arbitrary"), vmem_limit_bytes=VMEM_LIMIT),
    )(og, osb, p, p, p, p, x, w_up_gla, w_up_sb, w_o, post_gain).reshape(batch, rows, cm)


def _out_proj(og, osb, p, x, w_up_gla, w_up_sb, w_o, post_gain, *, row_target):
    n = x.shape[0]
    tm = _row_tile(n, row_target)
    const = lambda i: (0, 0)
    cm = D_MODEL
    hw = cm // 2
    half = lambda c: pl.BlockSpec((tm, hw), lambda i: (i, c // hw))
    resident = lambda shape: pl.BlockSpec(shape, const, pipeline_mode=pl.Buffered(1))
    return pl.pallas_call(
        functools.partial(_out_proj_kernel, parts=OUT_ROW_PARTS),
        grid=(n // tm,),
        in_specs=[
            pl.BlockSpec((tm, GLA_V), lambda i: (i, 0)),
            pl.BlockSpec((tm, SB_W), lambda i: (i, 0)),
            half(COL_MG), half(COL_MG + hw), half(COL_MS), half(COL_MS + hw),
            pl.BlockSpec((tm, cm), lambda i: (i, 0)),
            resident((GLA_V, cm)),
            resident((SB_W, cm)),
            resident((cm, cm)),
            pl.BlockSpec((1, cm), const),
        ],
        out_specs=pl.BlockSpec((tm, cm), lambda i: (i, 0)),
        out_shape=jax.ShapeDtypeStruct((n, cm), F32),
        compiler_params=pltpu.CompilerParams(
            dimension_semantics=("arbitrary",), vmem_limit_bytes=VMEM_LIMIT),
    )(og, osb, p, p, p, p, x, w_up_gla, w_up_sb, w_o, post_gain)


def _pack_weights(pre_gain, w_in, w_a2, b_a, gla_gain, w_up_gla, w_up_sb, w_o, post_gain):
    sizes = (GLA_QK, GLA_QK, GLA_V, GLA_V, GLA_RANK, SB_W, SB_W, SB_W, SB_W, D_MODEL, D_MODEL)
    offs = [0]
    for s in sizes:
        offs.append(offs[-1] + s)
    r0, r1 = offs[4], offs[5]
    w_main = jnp.concatenate([w_in[:, :r0], w_in[:, r1:]], axis=1).astype(BF16)
    w_r = jnp.pad(w_in[:, r0:r1], ((0, 0), (0, LANES - GLA_RANK))).astype(BF16)
    w_a2p = jnp.pad(w_a2.astype(F32), ((0, LANES - GLA_RANK), (0, 0)))
    return dict(
        pre_gain=pre_gain.reshape(1, D_MODEL).astype(F32), w_main=w_main, w_r=w_r, w_a2p=w_a2p,
        b_a=b_a.reshape(1, GLA_QK).astype(F32), gla_gain=gla_gain.astype(F32),
        w_up_gla=w_up_gla.astype(BF16), w_up_sb=w_up_sb.astype(BF16), w_o=w_o.astype(BF16),
        post_gain=post_gain.reshape(1, D_MODEL).astype(F32))


def _layer(x, batch, past, s0, w, kv_prev, *, layer, depth, in_rows, out_rows, drop_lead=0):
    n = x.shape[0]
    T = n // batch
    p, k_all, v_all, la, la_min = _in_proj(
        x, w["pre_gain"], w["w_main"], w["w_r"], w["w_a2p"], w["b_a"], kv_prev,
        layer=layer, depth=depth, row_target=in_rows)
    safe = (jnp.min(la_min) >= GLA_SAFE_LOG_DECAY).astype(jnp.int32).reshape(1)
    p3 = p.reshape(batch, T, P_COLS)
    og, s_new = _gla(p3, la.reshape(batch, T, GLA_QK), s0, w["gla_gain"], safe)
    if past is None:
        osb = _sb_prompt(p3)
    else:
        osb = _sb_sample(p3, past[0], past[1], layer=layer)
    tail_w = (w["w_up_gla"], w["w_up_sb"], w["w_o"], w["post_gain"])
    og, osb = og.reshape(n, GLA_V), osb.reshape(n, SB_W)
    if drop_lead:
        y = _out_proj_tail(og, osb, p, x, *tail_w, batch=batch, lead=drop_lead, row_target=out_rows)
    else:
        y = _out_proj(og, osb, p, x, *tail_w, row_target=out_rows)
    return y, (k_all, v_all), s_new


def kernel(x_prompt, x_sample, cache_sb_k, cache_sb_v, state_gla, meta_tokens, pre_gain, w_in, w_a2, b_a,
           gla_gain, w_up_gla, w_up_sb, w_o, post_gain):
    B, seq, _ = x_prompt.shape
    Bs, seq_s, _ = x_sample.shape
    depth = w_in.shape[0]
    T = N_META + seq
    meta = jnp.broadcast_to(meta_tokens.astype(x_prompt.dtype)[None], (B, N_META, D_MODEL))
    hp = jnp.concatenate([meta, x_prompt], axis=1).reshape(B * T, D_MODEL)
    hs = x_sample.reshape(Bs * seq_s, D_MODEL)
    zero_state = jnp.zeros((B, GLA_HEADS, GLA_DK, GLA_DV), F32)
    cache_rows = (depth, Bs, cache_sb_k.shape[2] * SB_HEADS, SB_HEAD_DIM)
    cache_k2, cache_v2 = cache_sb_k.reshape(cache_rows), cache_sb_v.reshape(cache_rows)
    kv_p, kv_s, sp, ss_ = None, None, [], []
    for l in range(depth):
        w = _pack_weights(pre_gain[l], w_in[l], w_a2[l], b_a[l], gla_gain[l], w_up_gla[l], w_up_sb[l],
                          w_o[l], post_gain[l])
        hp, kv_p, s_ = _layer(hp, B, None, zero_state, w, kv_p, layer=l, depth=depth, in_rows=768,
                              out_rows=512, drop_lead=N_META if l == depth - 1 else 0)
        sp.append(s_)
        hs, kv_s, s_ = _layer(hs, Bs, (cache_k2, cache_v2), state_gla[l], w, kv_s,
                              layer=l, depth=depth, in_rows=512, out_rows=256)
        ss_.append(s_)
    y_prompt = hp
    y_sample = hs.reshape(Bs, seq_s, D_MODEL)
    shape_p = (depth, B, T, SB_HEADS, SB_HEAD_DIM)
    shape_s = (depth, Bs, seq_s, SB_HEADS, SB_HEAD_DIM)
    return (y_prompt, y_sample, kv_p[0].reshape(shape_p), kv_p[1].reshape(shape_p), jnp.stack(sp),
            kv_s[0].reshape(shape_s), kv_s[1].reshape(shape_s), jnp.stack(ss_))
```

```python
import functools

import jax
import jax.numpy as jnp
from jax import lax
from jax.experimental import pallas as pl
from jax.experimental.pallas import tpu as pltpu

F32 = jnp.float32
BF16 = jnp.bfloat16

D_MODEL = 2048
N_META = 16
EPS = 1e-6
GLA_HEADS = 4
GLA_DK = 128
GLA_DV = 256
GLA_RANK = 16
GLA_TAU = 16.0
SB_HEADS = 8
SB_HEAD_DIM = 128
GLA_QK = GLA_HEADS * GLA_DK
GLA_V = GLA_HEADS * GLA_DV
SB_W = SB_HEADS * SB_HEAD_DIM

LANES = 128
PROJ_TILE = 1024
P_COLS = 11 * PROJ_TILE
COL_GQ, COL_GK, COL_GV, COL_GZ, COL_SQ, COL_SK, COL_SV, COL_SZ, COL_MG, COL_MS = (
    0, 512, 1024, 2048, 3072, 4096, 5120, 6144, 7168, 9216)
OUT_ROW_PARTS = 2

GLA_CHUNK = 64
GLA_HEADS_PER_STEP = 4
GLA_SAFE_LOG_DECAY = -1.0
SB_BLOCK = 128
SB_QUERY_TILE = 64
SB_WINDOW = 256
SB_TILE_GROUP = 4
SB_HEADS_PER_STEP = 4
SB_STATIC_BLOCKS = 2
SB_STOP_LOG = -120.0
VMEM_LIMIT = 56 * 1024 * 1024


def _row_tile(n, target):
    best = None
    for d in range(16, min(n, target) + 1, 16):
        if n % d == 0:
            best = d
    return best if best is not None else n


def _sigmoid(x):
    return 1.0 / (1.0 + jnp.exp(-x))


def _log_sigmoid(x):
    return jnp.minimum(x, 0.0) - jnp.log(1.0 + jnp.exp(-jnp.abs(x)))


def _dot(a, b):
    return jnp.dot(a, b, preferred_element_type=F32)


def _dot_nt(a, b):
    return lax.dot_general(a, b, (((1,), (1,)), ((), ())), preferred_element_type=F32)


def _dot_tn(a, b):
    return lax.dot_general(a, b, (((0,), (0,)), ((), ())), preferred_element_type=F32)


def _split3(x):
    hi = x.astype(BF16)
    r = x - hi.astype(F32)
    mid = r.astype(BF16)
    lo = (r - mid.astype(F32)).astype(BF16)
    return hi, mid, lo


def _in_proj_kernel(x_ref, g_ref, w_ref, wr_ref, wa2_ref, ba_ref, *rest, sub, aliased):
    if aliased:
        rest = rest[2:]
    p_ref, k_ref, v_ref, la_ref, lamin_ref, h_ref = rest
    j = pl.program_id(1)
    tm = x_ref.shape[0]

    @pl.when(j == 0)
    def _():
        gain = g_ref[...]

        def normalize(r0, n):
            for s in range(r0, r0 + n, sub):
                x = x_ref[s:s + sub, :]
                ms = jnp.mean(x * x, axis=-1, keepdims=True)
                h_ref[s:s + sub, :] = (x * lax.rsqrt(ms + EPS) * gain).astype(BF16)

        parts = 3 if tm % (3 * sub) == 0 else 1
        rp = tm // parts
        groups = [pl.ds(i * rp, rp) for i in range(parts)]
        g_rs = []
        for i, r in enumerate(groups):
            normalize(i * rp, rp)
            g_rs.append(_dot(h_ref[r, :], wr_ref[...]))
        w2h, w2m, _ = _split3(wa2_ref[...])
        pieces = [_split3(g_r)[:2] for g_r in g_rs]
        xas = [_dot(hi, w2h) + _dot(hi, w2m) + _dot(mid, w2h) + ba_ref[...] for hi, mid in pieces]
        las = [_log_sigmoid(xa) * (1.0 / GLA_TAU) for xa in xas]
        for r, la in zip(groups, las):
            la_ref[r, :] = la
        low = functools.reduce(jnp.minimum, las)
        m = jnp.min(jnp.min(low, axis=0, keepdims=True), axis=1, keepdims=True)
        lamin_ref[...] = jnp.broadcast_to(m.reshape(1, 1, 1), lamin_ref.shape)

    acc = _dot(h_ref[...], w_ref[...])
    p_ref[...] = acc.astype(BF16)

    def heads_out(ref):
        for h in range(SB_HEADS):
            ref[pl.ds(h, tm, stride=SB_HEADS), :] = acc[:, h * SB_HEAD_DIM:(h + 1) * SB_HEAD_DIM]

    @pl.when(j == COL_SK // PROJ_TILE)
    def _():
        heads_out(k_ref)

    @pl.when(j == COL_SV // PROJ_TILE)
    def _():
        heads_out(v_ref)


def _in_proj(x, gain, w_main, w_r, w_a2p, b_a, kv_prev, *, layer, depth, row_target):
    n = x.shape[0]
    tm = _row_tile(n, row_target)
    nt = n // tm
    sub = _row_tile(tm, 64)
    grid = (nt, P_COLS // PROJ_TILE)
    const = lambda i, j: (0, 0)
    aliased = kv_prev is not None
    kv_spec = pl.BlockSpec((None, tm * SB_HEADS, SB_HEAD_DIM), lambda i, j: (layer, i, 0))
    kv_shape = jax.ShapeDtypeStruct((depth, n * SB_HEADS, SB_HEAD_DIM), F32)
    in_specs = [
        pl.BlockSpec((tm, D_MODEL), lambda i, j: (i, 0)),
        pl.BlockSpec((1, D_MODEL), const),
        pl.BlockSpec((D_MODEL, PROJ_TILE), lambda i, j: (0, j)),
        pl.BlockSpec((D_MODEL, LANES), const),
        pl.BlockSpec((LANES, GLA_QK), const),
        pl.BlockSpec((1, GLA_QK), const),
    ]
    args = [x, gain, w_main, w_r, w_a2p, b_a]
    aliases = {}
    if aliased:
        in_specs += [pl.BlockSpec(memory_space=pl.ANY), pl.BlockSpec(memory_space=pl.ANY)]
        args += list(kv_prev)
        aliases = {6: 1, 7: 2}
    return pl.pallas_call(
        functools.partial(_in_proj_kernel, sub=sub, aliased=aliased),
        grid=grid,
        in_specs=in_specs,
        out_specs=[
            pl.BlockSpec((tm, PROJ_TILE), lambda i, j: (i, j)),
            kv_spec,
            kv_spec,
            pl.BlockSpec((tm, GLA_QK), lambda i, j: (i, 0)),
            pl.BlockSpec((1, 8, LANES), lambda i, j: (i, 0, 0)),
        ],
        out_shape=[
            jax.ShapeDtypeStruct((n, P_COLS), BF16),
            kv_shape,
            kv_shape,
            jax.ShapeDtypeStruct((n, GLA_QK), F32),
            jax.ShapeDtypeStruct((nt, 8, LANES), F32),
        ],
        scratch_shapes=[pltpu.VMEM((tm, D_MODEL), BF16)],
        input_output_aliases=aliases,
        compiler_params=pltpu.CompilerParams(
            dimension_semantics=("arbitrary", "arbitrary"), vmem_limit_bytes=VMEM_LIMIT),
    )(*args)


def _gla_kernel(safe_ref, q_ref, k_ref, v_ref, z_ref, la_ref, s0_ref, gain_ref,
                o_ref, sout_ref, st_ref, oi_ref, qs_ref, bs_ref, *, chunk, hp):
    C = chunk
    T = q_ref.shape[0]
    n_full, rem = T // C, T % C
    safe = safe_ref[0] > 0

    row = lax.broadcasted_iota(jnp.int32, (C, C), 0)
    col = lax.broadcasted_iota(jnp.int32, (C, C), 1)
    causal = row >= col
    tri = jnp.where(causal, 1.0, 0.0).astype(BF16)
    tri3 = jnp.concatenate([tri, tri, tri], axis=1)
    row1 = lax.broadcasted_iota(jnp.int32, (C, 1), 0)

    for hh in range(hp):
        st_ref[hh] = s0_ref[hh].T

    def group(chunks, fast):
        items = [(hh, off, pad) for off, pad in chunks for hh in range(hp)]
        kcols = lambda hh: slice(hh * GLA_DK, (hh + 1) * GLA_DK)
        vcols = lambda hh: slice(hh * GLA_DV, (hh + 1) * GLA_DV)
        qs, ks, vs, las = [], [], [], []
        for hh, off, pad in items:
            rows = pl.ds(off, C)
            q = q_ref[rows, kcols(hh)].astype(F32) * (GLA_DK ** -0.5)
            k = k_ref[rows, kcols(hh)].astype(F32)
            v = v_ref[rows, vcols(hh)]
            la = la_ref[rows, kcols(hh)]
            if pad:
                valid = row1 >= pad
                la = jnp.where(valid, la, 0.0)
                k = jnp.where(valid, k, 0.0)
                v = jnp.where(valid, v, jnp.zeros_like(v))
            qs.append(q)
            ks.append(k)
            vs.append(v)
            las.append(jnp.concatenate(_split3(la), axis=0))
        bs = [_dot(tri3, la3) for la3 in las]
        qes, kds, ends = [], [], []
        for q, k, b in zip(qs, ks, bs):
            b_end = b[C - 1:C, :]
            qes.append((q * jnp.exp(b)).astype(BF16))
            kds.append((k * jnp.exp(b_end - b)).astype(BF16))
            ends.append(jnp.exp(b_end))
        if fast:
            kes = [(k * jnp.exp(-b)).astype(BF16) for k, b in zip(ks, bs)]
            ss = [jnp.where(causal, _dot_nt(qe, ke), 0.0).astype(BF16) for qe, ke in zip(qes, kes)]
            intra = [_dot(s, v) for s, v in zip(ss, vs)]
        else:
            intra = []
            for q, k, v, b in zip(qs, ks, vs, bs):
                qs_ref[...] = q
                bs_ref[...] = b
                vf = v.astype(F32)

                def body(t, carry, k=k, b=b, vf=vf):
                    bt = bs_ref[pl.ds(t, 1), :]
                    qt = qs_ref[pl.ds(t, 1), :]
                    diff = jnp.where(row1 <= t, bt - b, -jnp.inf)
                    w = jnp.sum(k * jnp.exp(diff) * qt, axis=-1, keepdims=True)
                    oi_ref[pl.ds(t, 1), :] = jnp.sum(w * vf, axis=0, keepdims=True)
                    return carry

                lax.fori_loop(0, C, body, 0)
                intra.append(oi_ref[...])
        kvs = [_dot_tn(v, kd) for v, kd in zip(vs, kds)]
        sts = [st_ref[hh] for hh in range(hp)]
        os_ = []
        for i, (hh, off, pad) in enumerate(items):
            os_.append(_dot_nt(qes[i], sts[hh].astype(BF16)) + intra[i])
            sts[hh] = sts[hh] * ends[i] + kvs[i]
        for hh in range(hp):
            st_ref[hh] = sts[hh]
        for (hh, off, pad), o in zip(items, os_):
            on = o * lax.rsqrt(jnp.mean(o * o, axis=-1, keepdims=True) + EPS) * gain_ref[hh]
            z = z_ref[pl.ds(off, C), vcols(hh)].astype(F32)
            out = (on * (z * _sigmoid(z))).astype(BF16)
            if pad:
                o_ref[pl.ds(off + pad, C - pad), vcols(hh)] = out[pad:, :]
            else:
                o_ref[pl.ds(off, C), vcols(hh)] = out

    def run(fast):
        per = max(c for c in (1, 2, 4) if n_full % c == 0)

        def body(c, carry):
            base = pl.multiple_of(c * (per * C), per * C)
            group([(base + i * C, 0) for i in range(per)], fast)
            return carry

        lax.fori_loop(0, n_full // per, body, 0)
        if rem:
            group([(T - C, C - rem)], fast)

    pl.when(safe)(lambda: run(True))
    pl.when(jnp.logical_not(safe))(lambda: run(False))
    for hh in range(hp):
        sout_ref[hh] = st_ref[hh].T


def _gla(p3, la3, s0, gla_gain, safe, *, chunk=GLA_CHUNK, hp=GLA_HEADS_PER_STEP):
    B, T, _ = p3.shape
    H = GLA_HEADS
    wk, wv = hp * GLA_DK, hp * GLA_DV
    kq, kk = COL_GQ // wk, COL_GK // wk
    kv, kz = COL_GV // wv, COL_GZ // wv
    grid_spec = pltpu.PrefetchScalarGridSpec(
        num_scalar_prefetch=1,
        grid=(B, H // hp),
        in_specs=[
            pl.BlockSpec((None, T, wk), lambda b, g, s: (b, 0, kq + g)),
            pl.BlockSpec((None, T, wk), lambda b, g, s: (b, 0, kk + g)),
            pl.BlockSpec((None, T, wv), lambda b, g, s: (b, 0, kv + g)),
            pl.BlockSpec((None, T, wv), lambda b, g, s: (b, 0, kz + g)),
            pl.BlockSpec((None, T, wk), lambda b, g, s: (b, 0, g)),
            pl.BlockSpec((None, hp, GLA_DK, GLA_DV), lambda b, g, s: (b, g, 0, 0)),
            pl.BlockSpec((hp, 1, GLA_DV), lambda b, g, s: (g, 0, 0)),
        ],
        out_specs=[
            pl.BlockSpec((None, T, wv), lambda b, g, s: (b, 0, g)),
            pl.BlockSpec((None, hp, GLA_DK, GLA_DV), lambda b, g, s: (b, g, 0, 0)),
        ],
        scratch_shapes=[
            pltpu.VMEM((hp, GLA_DV, GLA_DK), F32),
            pltpu.VMEM((chunk, GLA_DV), F32),
            pltpu.VMEM((chunk, GLA_DK), F32),
            pltpu.VMEM((chunk, GLA_DK), F32),
        ],
    )
    return pl.pallas_call(
        functools.partial(_gla_kernel, chunk=chunk, hp=hp),
        grid_spec=grid_spec,
        out_shape=[
            jax.ShapeDtypeStruct((B, T, GLA_V), BF16),
            jax.ShapeDtypeStruct((B, H, GLA_DK, GLA_DV), F32),
        ],
        compiler_params=pltpu.CompilerParams(
            dimension_semantics=("arbitrary", "arbitrary"), vmem_limit_bytes=VMEM_LIMIT),
    )(safe, p3, p3, p3, p3, la3, s0, gla_gain.reshape(H, 1, GLA_DV))


def _strict_upper(n):
    j = lax.broadcasted_iota(jnp.int32, (n, n), 0)
    s = lax.broadcasted_iota(jnp.int32, (n, n), 1)
    return jnp.where(j > s, 1.0, 0.0).astype(BF16)


def _sb_step(q, kb, vb, upper, mask, carry, acc):
    z = _dot_nt(q, kb)
    lb = _log_sigmoid(z)
    lom = lb - z
    if mask is not None:
        lom = jnp.where(mask, lom, 0.0)
    hi = lom.astype(BF16)
    lo = (lom - hi.astype(F32)).astype(BF16)
    tail = _dot(hi, upper) + _dot(lo, upper) + carry
    w = jnp.exp(lb + tail)
    if mask is not None:
        w = jnp.where(mask, w, 0.0)
    acc = acc + _dot(w.astype(BF16), vb)
    carry = carry + jnp.sum(lom, axis=-1, keepdims=True)
    return carry, acc


def _sb_prompt_kernel(q_ref, k_ref, v_ref, z_ref, o_ref, c_ref, a_ref, *, blk, tq, win, hp, group):
    T = q_ref.shape[0]
    d = SB_HEAD_DIM
    back = win - tq
    n_tiles, rem = T // tq, T % tq
    upper = _strict_upper(blk)
    rowq = lax.broadcasted_iota(jnp.int32, (tq, blk), 0)
    colk = lax.broadcasted_iota(jnp.int32, (tq, blk), 1)
    scale = SB_HEAD_DIM ** -0.5
    heads = [slice(hh * d, (hh + 1) * d) for hh in range(hp)]

    def block_mask(delta, a):
        if (a + 1) * blk - 1 < delta:
            return None
        if a * blk >= delta + tq - 1:
            return False
        return a * blk + colk < delta + rowq

    def tiles(specs):
        units = [(r0, k0, delta, cols) for r0, k0, delta in specs for cols in heads]
        masks = [[m for m in (block_mask(delta, a) for a in range(win // blk)) if m is not False]
                 for _, _, delta, _ in units]
        krows = [pl.ds(k0, len(ms) * blk) for (_, k0, _, _), ms in zip(units, masks)]
        qs = [(q_ref[pl.ds(r0, tq), cols].astype(F32) * scale).astype(BF16) for r0, _, _, cols in units]
        zs = [_dot_nt(q, k_ref[kr, cols]) for q, kr, (_, _, _, cols) in zip(qs, krows, units)]
        lbs, hls, sums = [], [], []
        for z, ms in zip(zs, masks):
            lb = _log_sigmoid(z)
            lom = lb - z
            hl, rs = [], []
            for a, m in enumerate(ms):
                lom_a = lom[:, a * blk:(a + 1) * blk]
                if m is not None:
                    lom_a = jnp.where(m, lom_a, 0.0)
                hl.append(lom_a.astype(BF16))
                rs.append(jnp.sum(lom_a, axis=-1, keepdims=True))
            lbs.append(lb)
            hls.append(hl)
            sums.append(rs)
        tails = [[_dot(hl_a, upper) for hl_a in hl] for hl in hls]
        ws, carries = [], []
        for lb, tl, rs, ms in zip(lbs, tails, sums, masks):
            carry = jnp.zeros((tq, 1), F32)
            w = [None] * len(ms)
            for a in range(len(ms) - 1, -1, -1):
                w_a = jnp.exp(lb[:, a * blk:(a + 1) * blk] + tl[a] + carry)
                if ms[a] is not None:
                    w_a = jnp.where(ms[a], w_a, 0.0)
                w[a] = w_a.astype(BF16)
                carry = carry + rs[a]
            ws.append(jnp.concatenate(w, axis=1) if len(w) > 1 else w[0])
            carries.append(carry)
        accs = [_dot(w, v_ref[kr, cols]) for w, kr, (_, _, _, cols) in zip(ws, krows, units)]

        def emit(u, acc):
            r0, _, _, cols = units[u]
            zg = z_ref[pl.ds(r0, tq), cols].astype(F32)
            o_ref[pl.ds(r0, tq), cols] = (acc * (zg * _sigmoid(zg))).astype(BF16)

        for u in range(len(units)):
            emit(u, accs[u])
        if all(isinstance(k0, int) and k0 == 0 for _, k0, _, _ in units):
            return
        worst = functools.reduce(jnp.maximum, carries)

        @pl.when(jnp.max(worst) > SB_STOP_LOG)
        def _():
            for u in range(len(units)):
                c_ref[u] = carries[u]
                a_ref[u] = accs[u]
            for u, (r0, k0, _, cols) in enumerate(units):
                n_prev = k0 // blk
                head = k0 - n_prev * blk
                q = qs[u]

                def cond(s):
                    return jnp.logical_and(s[0] < n_prev, s[1] > SB_STOP_LOG)

                def body(s, u=u, cols=cols, q=q, k0=k0):
                    ks = pl.ds(pl.multiple_of(k0 - (s[0] + 1) * blk, 16), blk)
                    c, a = _sb_step(q, k_ref[ks, cols], v_ref[ks, cols], upper, None, c_ref[u], a_ref[u])
                    c_ref[u] = c
                    a_ref[u] = a
                    return s[0] + 1, jnp.max(c)

                _, cmax = lax.while_loop(cond, body, (jnp.int32(0), jnp.max(carries[u])))
                if not (isinstance(head, int) and head == 0):
                    @pl.when(jnp.logical_and(cmax > SB_STOP_LOG, head > 0))
                    def _(u=u, cols=cols, q=q, head=head):
                        _, a = _sb_step(q, k_ref[0:blk, cols], v_ref[0:blk, cols], upper, colk < head,
                                        c_ref[u], a_ref[u])
                        a_ref[u] = a
                emit(u, a_ref[u])

    n_lead = min(n_tiles, -(-back // tq))
    static = [(i * tq, 0, i * tq) for i in range(n_lead)]
    first = n_lead + (n_tiles - n_lead) % group
    static += [(i * tq, i * tq - back, back) for i in range(n_lead, first)]
    if rem:
        static.append((T - tq, T - tq - back, back))
    for g in range(0, len(static), group):
        tiles(static[g:g + group])

    def body(m, carry):
        r0s = [pl.multiple_of((first + m * group + g) * tq, tq) for g in range(group)]
        tiles([(r0, r0 - back, back) for r0 in r0s])
        return carry

    lax.fori_loop(0, (n_tiles - first) // group, body, 0)


def _sb_prompt(p3, *, blk=SB_BLOCK, tq=SB_QUERY_TILE, win=SB_WINDOW, hp=SB_HEADS_PER_STEP, group=SB_TILE_GROUP):
    B, T, _ = p3.shape
    assert T >= win and win % blk == 0 and tq <= blk and tq % 16 == 0
    w = hp * SB_HEAD_DIM
    spec = lambda c: pl.BlockSpec((None, T, w), lambda b, g: (b, 0, c // w + g))
    return pl.pallas_call(
        functools.partial(_sb_prompt_kernel, blk=blk, tq=tq, win=win, hp=hp, group=group),
        grid=(B, SB_HEADS // hp),
        in_specs=[spec(COL_SQ), spec(COL_SK), spec(COL_SV), spec(COL_SZ)],
        out_specs=pl.BlockSpec((None, T, w), lambda b, g: (b, 0, g)),
        out_shape=jax.ShapeDtypeStruct((B, T, SB_W), BF16),
        scratch_shapes=[pltpu.VMEM((group * hp, tq, 1), F32), pltpu.VMEM((group * hp, tq, SB_HEAD_DIM), F32)],
        compiler_params=pltpu.CompilerParams(
            dimension_semantics=("arbitrary", "arbitrary"), vmem_limit_bytes=VMEM_LIMIT),
    )(p3, p3, p3, p3)


def _sb_sample_kernel(q_ref, k_ref, v_ref, z_ref, pk_ref, pv_ref, pk_hbm, pv_hbm, o_ref,
                      kbuf, vbuf, c_ref, a_ref, sem, *, blk, layer):
    b = pl.program_id(0)
    tq = q_ref.shape[0]
    d = SB_HEAD_DIM
    nh = SB_HEADS
    tail_len = pk_ref.shape[0] // nh
    n_stat = tail_len // blk
    past_len = pk_hbm.shape[2] // nh
    n_past = past_len // blk
    row = lax.broadcasted_iota(jnp.int32, (tq, tq), 0)
    col = lax.broadcasted_iota(jnp.int32, (tq, tq), 1)
    upper_q = _strict_upper(tq)
    upper = _strict_upper(blk)
    scale = SB_HEAD_DIM ** -0.5

    def fetch(s0):
        src = pl.ds(pl.multiple_of(s0 * nh, blk * nh), blk * nh)
        ck = pltpu.make_async_copy(pk_hbm.at[layer, b, src], kbuf, sem.at[0])
        cv = pltpu.make_async_copy(pv_hbm.at[layer, b, src], vbuf, sem.at[1])
        ck.start()
        cv.start()
        ck.wait()
        cv.wait()

    heads = [slice(h * d, (h + 1) * d) for h in range(SB_HEADS)]
    strict = col < row
    nb = n_stat + 1
    past_rows = lambda h, a: pl.ds((tail_len - (n_stat - a) * blk) * nh + h, blk, stride=nh)
    kbs = [[pk_ref[past_rows(h, a), :].astype(BF16) for a in range(n_stat)] + [k_ref[:, cols]]
           for h, cols in enumerate(heads)]
    vbs = [[pv_ref[past_rows(h, a), :].astype(BF16) for a in range(n_stat)] + [v_ref[:, cols]]
           for h, cols in enumerate(heads)]
    uppers = [jnp.concatenate([upper, upper], axis=0)] * n_stat + [jnp.concatenate([upper_q, upper_q], axis=0)]
    qs = [(q_ref[:, cols].astype(F32) * scale).astype(BF16) for cols in heads]
    zs = [[_dot_nt(q, kb) for kb in kb_h] for q, kb_h in zip(qs, kbs)]
    lbs, hls, sums = [], [], []
    for z_h in zs:
        lb_h, hl_h, rs_h = [], [], []
        for a, z in enumerate(z_h):
            lb = _log_sigmoid(z)
            lom = lb - z
            if a == nb - 1:
                lom = jnp.where(strict, lom, 0.0)
            hi = lom.astype(BF16)
            lo = (lom - hi.astype(F32)).astype(BF16)
            lb_h.append(lb)
            hl_h.append(jnp.concatenate([hi, lo], axis=1))
            rs_h.append(jnp.sum(lom, axis=-1, keepdims=True))
        lbs.append(lb_h)
        hls.append(hl_h)
        sums.append(rs_h)
    tails = [[_dot(hl, up) for hl, up in zip(hl_h, uppers)] for hl_h in hls]
    ws, carries = [], []
    for lb_h, tl_h, rs_h in zip(lbs, tails, sums):
        carry = jnp.zeros((tq, 1), F32)
        w_h = [None] * nb
        for a in range(nb - 1, -1, -1):
            w = jnp.exp(lb_h[a] + tl_h[a] + carry)
            if a == nb - 1:
                w = jnp.where(strict, w, 0.0)
            w_h[a] = w.astype(BF16)
            carry = carry + rs_h[a]
        ws.append(w_h)
        carries.append(carry)
    accs = [functools.reduce(jnp.add, [_dot(w, vb) for w, vb in zip(w_h, vb_h)]) for w_h, vb_h in zip(ws, vbs)]

    def emit(h, acc):
        zg = z_ref[:, heads[h]].astype(F32)
        o_ref[:, heads[h]] = (acc * (zg * _sigmoid(zg))).astype(BF16)

    for h in range(SB_HEADS):
        emit(h, accs[h])
    if n_past == n_stat:
        return
    worst = functools.reduce(jnp.maximum, carries)

    @pl.when(jnp.max(worst) > SB_STOP_LOG)
    def _():
        for h in range(SB_HEADS):
            c_ref[h] = carries[h]
            a_ref[h] = accs[h]
        for h in range(SB_HEADS):
            def cond(s):
                return jnp.logical_and(s[0] < n_past, s[1] > SB_STOP_LOG)

            def body(s, h=h, q=qs[h]):
                fetch(pl.multiple_of(past_len - (s[0] + 1) * blk, blk))
                hs = pl.ds(h, blk, stride=nh)
                c, a = _sb_step(q, kbuf[hs, :].astype(BF16), vbuf[hs, :].astype(BF16), upper, None,
                                c_ref[h], a_ref[h])
                c_ref[h] = c
                a_ref[h] = a
                return s[0] + 1, jnp.max(c)

            lax.while_loop(cond, body, (jnp.int32(n_stat), jnp.max(carries[h])))
            emit(h, a_ref[h])


def _sb_sample(p3, past_k, past_v, *, layer, blk=SB_BLOCK, n_static=SB_STATIC_BLOCKS):
    B, T, _ = p3.shape
    nh = SB_HEADS
    past_len = past_k.shape[2] // nh
    assert past_len % blk == 0 and T % 16 == 0
    n_static = min(n_static, past_len // blk)
    tail = n_static * blk
    assert past_len % tail == 0
    spec = lambda c: pl.BlockSpec((None, T, SB_W), lambda b: (b, 0, c // SB_W))
    past_tail = pl.BlockSpec((None, None, tail * nh, SB_HEAD_DIM),
                             lambda b: (layer, b, past_len // tail - 1, 0))
    hbm = pl.BlockSpec(memory_space=pl.ANY)
    return pl.pallas_call(
        functools.partial(_sb_sample_kernel, blk=blk, layer=layer),
        grid=(B,),
        in_specs=[spec(COL_SQ), spec(COL_SK), spec(COL_SV), spec(COL_SZ), past_tail, past_tail, hbm, hbm],
        out_specs=pl.BlockSpec((None, T, SB_W), lambda b: (b, 0, 0)),
        out_shape=jax.ShapeDtypeStruct((B, T, SB_W), BF16),
        scratch_shapes=[
            pltpu.VMEM((blk * nh, SB_HEAD_DIM), F32),
            pltpu.VMEM((blk * nh, SB_HEAD_DIM), F32),
            pltpu.VMEM((SB_HEADS, T, 1), F32),
            pltpu.VMEM((SB_HEADS, T, SB_HEAD_DIM), F32),
            pltpu.SemaphoreType.DMA((2,)),
        ],
        compiler_params=pltpu.CompilerParams(
            dimension_semantics=("arbitrary",), vmem_limit_bytes=VMEM_LIMIT),
    )(p3, p3, p3, p3, past_k, past_v, past_k, past_v)


def _out_proj_kernel(og_ref, os_ref, mg0_ref, mg1_ref, ms0_ref, ms1_ref, x_ref, wg_ref, ws_ref, wo_ref,
                     g_ref, y_ref, *, parts):
    rp = x_ref.shape[0] // parts
    groups = [pl.ds(i * rp, rp) for i in range(parts)]
    hw = D_MODEL // 2
    gate = lambda ref, r: _sigmoid(ref[r, :].astype(F32))
    y_gla = [_dot(og_ref[r, :], wg_ref[...]) for r in groups]
    y_sb = [_dot(os_ref[r, :], ws_ref[...]) for r in groups]
    merged = []
    for r, yg, ys in zip(groups, y_gla, y_sb):
        lo = gate(mg0_ref, r) * yg[:, :hw] + gate(ms0_ref, r) * ys[:, :hw]
        hi = gate(mg1_ref, r) * yg[:, hw:] + gate(ms1_ref, r) * ys[:, hw:]
        merged.append(jnp.concatenate([lo, hi], axis=1).astype(BF16))
    us = [_dot(m, wo_ref[...]) for m in merged]
    for r, u in zip(groups, us):
        un = u * lax.rsqrt(jnp.mean(u * u, axis=-1, keepdims=True) + EPS) * g_ref[...]
        y_ref[r, :] = x_ref[r, :] + un


def _out_proj_tail(og, osb, p, x, w_up_gla, w_up_sb, w_o, post_gain, *, batch, lead, row_target):
    T = x.shape[0] // batch
    rows = T - lead
    tm = _row_tile(rows, row_target)
    nt = rows // tm
    const = lambda b, i: (0, 0)
    cm = D_MODEL
    assert T % 16 == 0 and lead % 16 == 0 and tm % 16 == 0
    at = lambda col: (lambda b, i: (pl.multiple_of(b * T + lead + i * tm, 16), col))
    resident = lambda shape: pl.BlockSpec(shape, const, pipeline_mode=pl.Buffered(1))
    tile = lambda cols: (pl.Element(tm), pl.Element(cols))
    hw = cm // 2
    return pl.pallas_call(
        functools.partial(_out_proj_kernel, parts=OUT_ROW_PARTS),
        grid=(batch, nt),
        in_specs=[
            pl.BlockSpec(tile(GLA_V), at(0)),
            pl.BlockSpec(tile(SB_W), at(0)),
            pl.BlockSpec(tile(hw), at(COL_MG)),
            pl.BlockSpec(tile(hw), at(COL_MG + hw)),
            pl.BlockSpec(tile(hw), at(COL_MS)),
            pl.BlockSpec(tile(hw), at(COL_MS + hw)),
            pl.BlockSpec(tile(cm), at(0)),
            resident((GLA_V, cm)),
            resident((SB_W, cm)),
            resident((cm, cm)),
            pl.BlockSpec((1, cm), const),
        ],
        out_specs=pl.BlockSpec((tm, cm), lambda b, i: (b * nt + i, 0)),
        out_shape=jax.ShapeDtypeStruct((batch * rows, cm), F32),
        compiler_params=pltpu.CompilerParams(
            dimension_semantics=("arbitrary", "arbitrary"), vmem_limit_bytes=VMEM_LIMIT),
    )(og, osb, p, p, p, p, x, w_up_gla, w_up_sb, w_o, post_gain).reshape(batch, rows, cm)


def _out_proj(og, osb, p, x, w_up_gla, w_up_sb, w_o, post_gain, *, row_target):
    n = x.shape[0]
    tm = _row_tile(n, row_target)
    const = lambda i: (0, 0)
    cm = D_MODEL
    hw = cm // 2
    half = lambda c: pl.BlockSpec((tm, hw), lambda i: (i, c // hw))
    resident = lambda shape: pl.BlockSpec(shape, const, pipeline_mode=pl.Buffered(1))
    return pl.pallas_call(
        functools.partial(_out_proj_kernel, parts=OUT_ROW_PARTS),
        grid=(n // tm,),
        in_specs=[
            pl.BlockSpec((tm, GLA_V), lambda i: (i, 0)),
            pl.BlockSpec((tm, SB_W), lambda i: (i, 0)),
            half(COL_MG), half(COL_MG + hw), half(COL_MS), half(COL_MS + hw),
            pl.BlockSpec((tm, cm), lambda i: (i, 0)),
            resident((GLA_V, cm)),
            resident((SB_W, cm)),
            resident((cm, cm)),
            pl.BlockSpec((1, cm), const),
        ],
        out_specs=pl.BlockSpec((tm, cm), lambda i: (i, 0)),
        out_shape=jax.ShapeDtypeStruct((n, cm), F32),
        compiler_params=pltpu.CompilerParams(
            dimension_semantics=("arbitrary",), vmem_limit_bytes=VMEM_LIMIT),
    )(og, osb, p, p, p, p, x, w_up_gla, w_up_sb, w_o, post_gain)


def _pack_weights(pre_gain, w_in, w_a2, b_a, gla_gain, w_up_gla, w_up_sb, w_o, post_gain):
    sizes = (GLA_QK, GLA_QK, GLA_V, GLA_V, GLA_RANK, SB_W, SB_W, SB_W, SB_W, D_MODEL, D_MODEL)
    offs = [0]
    for s in sizes:
        offs.append(offs[-1] + s)
    r0, r1 = offs[4], offs[5]
    w_main = jnp.concatenate([w_in[:, :r0], w_in[:, r1:]], axis=1).astype(BF16)
    w_r = jnp.pad(w_in[:, r0:r1], ((0, 0), (0, LANES - GLA_RANK))).astype(BF16)
    w_a2p = jnp.pad(w_a2.astype(F32), ((0, LANES - GLA_RANK), (0, 0)))
    return dict(
        pre_gain=pre_gain.reshape(1, D_MODEL).astype(F32), w_main=w_main, w_r=w_r, w_a2p=w_a2p,
        b_a=b_a.reshape(1, GLA_QK).astype(F32), gla_gain=gla_gain.astype(F32),
        w_up_gla=w_up_gla.astype(BF16), w_up_sb=w_up_sb.astype(BF16), w_o=w_o.astype(BF16),
        post_gain=post_gain.reshape(1, D_MODEL).astype(F32))


def _layer(x, batch, past, s0, w, kv_prev, *, layer, depth, in_rows, out_rows, drop_lead=0):
    n = x.shape[0]
    T = n // batch
    p, k_all, v_all, la, la_min = _in_proj(
        x, w["pre_gain"], w["w_main"], w["w_r"], w["w_a2p"], w["b_a"], kv_prev,
        layer=layer, depth=depth, row_target=in_rows)
    safe = (jnp.min(la_min) >= GLA_SAFE_LOG_DECAY).astype(jnp.int32).reshape(1)
    p3 = p.reshape(batch, T, P_COLS)
    og, s_new = _gla(p3, la.reshape(batch, T, GLA_QK), s0, w["gla_gain"], safe)
    if past is None:
        osb = _sb_prompt(p3)
    else:
        osb = _sb_sample(p3, past[0], past[1], layer=layer)
    tail_w = (w["w_up_gla"], w["w_up_sb"], w["w_o"], w["post_gain"])
    og, osb = og.reshape(n, GLA_V), osb.reshape(n, SB_W)
    if drop_lead:
        y = _out_proj_tail(og, osb, p, x, *tail_w, batch=batch, lead=drop_lead, row_target=out_rows)
    else:
        y = _out_proj(og, osb, p, x, *tail_w, row_target=out_rows)
    return y, (k_all, v_all), s_new


def kernel(x_prompt, x_sample, cache_sb_k, cache_sb_v, state_gla, meta_tokens, pre_gain, w_in, w_a2, b_a,
           gla_gain, w_up_gla, w_up_sb, w_o, post_gain):
    B, seq, _ = x_prompt.shape
    Bs, seq_s, _ = x_sample.shape
    depth = w_in.shape[0]
    T = N_META + seq
    meta = jnp.broadcast_to(meta_tokens.astype(x_prompt.dtype)[None], (B, N_META, D_MODEL))
    hp = jnp.concatenate([meta, x_prompt], axis=1).reshape(B * T, D_MODEL)
    hs = x_sample.reshape(Bs * seq_s, D_MODEL)
    zero_state = jnp.zeros((B, GLA_HEADS, GLA_DK, GLA_DV), F32)
    cache_rows = (depth, Bs, cache_sb_k.shape[2] * SB_HEADS, SB_HEAD_DIM)
    cache_k2, cache_v2 = cache_sb_k.reshape(cache_rows), cache_sb_v.reshape(cache_rows)
    kv_p, kv_s, sp, ss_ = None, None, [], []
    for l in range(depth):
        w = _pack_weights(pre_gain[l], w_in[l], w_a2[l], b_a[l], gla_gain[l], w_up_gla[l], w_up_sb[l],
                          w_o[l], post_gain[l])
        hp, kv_p, s_ = _layer(hp, B, None, zero_state, w, kv_p, layer=l, depth=depth, in_rows=768,
                              out_rows=512, drop_lead=N_META if l == depth - 1 else 0)
        sp.append(s_)
        hs, kv_s, s_ = _layer(hs, Bs, (cache_k2, cache_v2), state_gla[l], w, kv_s,
                              layer=l, depth=depth, in_rows=512, out_rows=256)
        ss_.append(s_)
    y_prompt = hp
    y_sample = hs.reshape(Bs, seq_s, D_MODEL)
    shape_p = (depth, B, T, SB_HEADS, SB_HEAD_DIM)
    shape_s = (depth, Bs, seq_s, SB_HEADS, SB_HEAD_DIM)
    return (y_prompt, y_sample, kv_p[0].reshape(shape_p), kv_p[1].reshape(shape_p), jnp.stack(sp),
            kv_s[0].reshape(shape_s), kv_s[1].reshape(shape_s), jnp.stack(ss_))
```
